```python
import jax, jax.numpy as jnp
from jax import lax
import numpy as np

D_MODEL = 1024
BATCH = 8
SEQ = 4096
DEPTH = 2

CHUNK = 64
N_MIXERS = 2
N_MLSTM = (DEPTH + 1) // 2
N_RGLRU = DEPTH // 2

ML_HEADS = 8
ML_QK_DIM = D_MODEL // 2 // ML_HEADS
ML_V_DIM = D_MODEL // ML_HEADS
ML_QK = ML_HEADS * ML_QK_DIM
ML_V = ML_HEADS * ML_V_DIM
ML_IN = 2 * ML_QK + 2 * ML_V + 2 * ML_HEADS
GATE_CAP = 15.0
ML_M_INIT = -1e30

LRU_WIDTH = D_MODEL
LRU_BLOCKS = 4
LRU_BLOCK = LRU_WIDTH // LRU_BLOCKS
CONV_WIDTH = 4
LRU_C = 8.0

_FF_RAW = -(-8 * D_MODEL // 3)
D_FF = -(-_FF_RAW // 256) * 256

EPS = 1e-6

kernel_name = "hybrid_mlstm_rglru_trunk"


def rms_norm(x, g):
    xf = x.astype(jnp.float32)
    y = xf * lax.rsqrt(jnp.mean(xf * xf, axis=-1, keepdims=True) + EPS)
    return (y * g.astype(jnp.float32)).astype(x.dtype)


def soft_cap(z):
    return GATE_CAP * jnp.tanh(z / GATE_CAP)


def mlstm_mixer(x, w_in, b_if, head_norm, w_out):
    B, S, _ = x.shape
    H, dk, dv, L = ML_HEADS, ML_QK_DIM, ML_V_DIM, CHUNK
    NC = S // L
    proj = x @ w_in
    q, k, v, o, g = jnp.split(proj, [ML_QK, 2 * ML_QK, 2 * ML_QK + ML_V, 2 * ML_QK + 2 * ML_V], axis=-1)
    g = soft_cap(g.astype(jnp.float32) + b_if.astype(jnp.float32))
    log_i = g[..., :H]
    log_f = jax.nn.log_sigmoid(g[..., H:])

    def to_chunks(t, d):
        return t.astype(jnp.float32).reshape(B, NC, L, H, d).transpose(1, 0, 3, 2, 4)

    qc = to_chunks(q, dk) * (dk ** -0.5)
    kc = to_chunks(k, dk)
    vc = to_chunks(v, dv)
    li = log_i.reshape(B, NC, L, H).transpose(1, 0, 3, 2)
    lf = log_f.reshape(B, NC, L, H).transpose(1, 0, 3, 2)
    causal = jnp.tril(jnp.ones((L, L), dtype=bool))

    def step(carry, inp):
        C, n, m = carry
        qb, kb, vb, lib, lfb = inp
        b = jnp.cumsum(lfb, axis=-1)
        D = b[..., :, None] - b[..., None, :] + lib[..., None, :]
        D = jnp.where(causal, D, -jnp.inf)
        inter = b + m[..., None]
        m_t = jnp.maximum(jnp.max(D, axis=-1), inter)
        P = jnp.exp(D - m_t[..., None]) * jnp.einsum('bhld,bhsd->bhls', qb, kb)
        w_inter = jnp.exp(inter - m_t)
        num = jnp.einsum('bhls,bhsv->bhlv', P, vb) + w_inter[..., None] * jnp.einsum('bhld,bhdv->bhlv', qb, C)
        den = jnp.sum(P, axis=-1) + w_inter * jnp.einsum('bhld,bhd->bhl', qb, n)
        den = jnp.maximum(jnp.abs(den), jnp.exp(-m_t))
        h = num / den[..., None]
        g_tot = b[..., -1]
        a = g_tot[..., None] - b + lib
        m_new = jnp.maximum(g_tot + m, jnp.max(a, axis=-1))
        decay = jnp.exp(g_tot + m - m_new)
        wk = jnp.exp(a - m_new[..., None])[..., None] * kb
        C_new = decay[..., None, None] * C + jnp.einsum('bhsd,bhsv->bhdv', wk, vb)
        n_new = decay[..., None] * n + jnp.sum(wk, axis=2)
        return (C_new, n_new, m_new), h

    carry0 = (jnp.zeros((B, H, dk, dv), jnp.float32),
              jnp.zeros((B, H, dk), jnp.float32),
              jnp.full((B, H), ML_M_INIT, jnp.float32))
    _, hs = lax.scan(step, carry0, (qc, kc, vc, li, lf))
    hs = hs.transpose(1, 0, 3, 2, 4).reshape(B, S, H, dv)
    hs = hs * lax.rsqrt(jnp.mean(hs * hs, axis=-1, keepdims=True) + EPS)
    hs = hs * head_norm.astype(jnp.float32).reshape(H, dv)
    y = hs.reshape(B, S, ML_V) * jax.nn.sigmoid(o.astype(jnp.float32))
    return y.astype(x.dtype) @ w_out


def causal_depthwise_conv(u, w, bias):
    C = u.shape[-1]
    out = lax.conv_general_dilated(
        u, w[:, None, :].astype(u.dtype), window_strides=(1,), padding=[(CONV_WIDTH - 1, 0)],
        dimension_numbers=('NWC', 'WIO', 'NWC'), feature_group_count=C)
    return out + bias.astype(u.dtype)


def rglru_mixer(x, w_in, conv_w, conv_b, w_gate_a, b_gate_a, w_gate_x, b_gate_x, lam, w_out):
    B, S, _ = x.shape
    proj = x @ w_in
    gate_branch, u = proj[..., :LRU_WIDTH], proj[..., LRU_WIDTH:]
    u = causal_depthwise_conv(u, conv_w, conv_b)
    ub = u.reshape(B, S, LRU_BLOCKS, LRU_BLOCK)
    r = jax.nn.sigmoid((jnp.einsum('bsnc,ncd->bsnd', ub, w_gate_a).reshape(B, S, LRU_WIDTH)
                        + b_gate_a).astype(jnp.float32))
    i = jax.nn.sigmoid((jnp.einsum('bsnc,ncd->bsnd', ub, w_gate_x).reshape(B, S, LRU_WIDTH)
                        + b_gate_x).astype(jnp.float32))
    log_a = LRU_C * r * jax.nn.log_sigmoid(lam.astype(jnp.float32))
    a = jnp.exp(log_a)
    mult = jnp.sqrt(-jnp.expm1(2.0 * log_a))
    b = mult * i * u.astype(jnp.float32)

    def combine(lhs, rhs):
        a1, b1 = lhs
        a2, b2 = rhs
        return a1 * a2, a2 * b1 + b2

    _, h = lax.associative_scan(combine, (a, b), axis=1)
    y = h * jax.nn.gelu(gate_branch.astype(jnp.float32))
    return y.astype(x.dtype) @ w_out


def swiglu(x, w_gate, w_up, w_down):
    return (jax.nn.silu(x @ w_gate) * (x @ w_up)) @ w_down


def setup_inputs(seed: int = 0) -> dict:
    key = jax.random.key(seed)
    ks = jax.random.split(key, 24)
    f32 = jnp.float32

    def nrm(k, shape, fan_in):
        return jax.random.normal(k, shape, f32) * (fan_in ** -0.5)

    def gain(k, shape):
        return 1.0 + 0.05 * jax.random.normal(k, shape, f32)

    x = jax.random.normal(ks[0], (BATCH, SEQ, D_MODEL), f32)
    ml_w_in = nrm(ks[1], (N_MLSTM, D_MODEL, ML_IN), D_MODEL)
    b_i = 0.1 * jax.random.normal(ks[2], (N_MLSTM, ML_HEADS), f32)
    b_f = 3.0 + 3.0 * jax.random.uniform(ks[3], (N_MLSTM, ML_HEADS), f32)
    ml_b_if = jnp.concatenate([b_i, b_f], axis=-1)
    ml_head_norm = gain(ks[4], (N_MLSTM, ML_V))
    ml_w_out = nrm(ks[5], (N_MLSTM, ML_V, D_MODEL), ML_V)
    lru_w_in = nrm(ks[6], (N_RGLRU, D_MODEL, 2 * LRU_WIDTH), D_MODEL)
    lru_conv_w = nrm(ks[7], (N_RGLRU, CONV_WIDTH, LRU_WIDTH), CONV_WIDTH)
    lru_conv_b = 0.02 * jax.random.normal(ks[8], (N_RGLRU, LRU_WIDTH), f32)
    lru_w_gate_a = nrm(ks[9], (N_RGLRU, LRU_BLOCKS, LRU_BLOCK, LRU_BLOCK), LRU_BLOCK)
    lru_b_gate_a = 0.02 * jax.random.normal(ks[10], (N_RGLRU, LRU_WIDTH), f32)
    lru_w_gate_x = nrm(ks[11], (N_RGLRU, LRU_BLOCKS, LRU_BLOCK, LRU_BLOCK), LRU_BLOCK)
    lru_b_gate_x = 0.02 * jax.random.normal(ks[12], (N_RGLRU, LRU_WIDTH), f32)
    u = jax.random.uniform(ks[13], (N_RGLRU, LRU_WIDTH), f32, minval=0.9, maxval=0.999)
    lru_lambda = jnp.log(u) - jnp.log1p(-u)
    lru_w_out = nrm(ks[14], (N_RGLRU, LRU_WIDTH, D_MODEL), LRU_WIDTH)
    norm_pre_mix = gain(ks[15], (DEPTH, D_MODEL))
    norm_post_mix = gain(ks[16], (DEPTH, D_MODEL))
    norm_pre_ffn = gain(ks[17], (DEPTH, D_MODEL))
    norm_post_ffn = gain(ks[18], (DEPTH, D_MODEL))
    ffn_w_gate = nrm(ks[19], (DEPTH, D_MODEL, D_FF), D_MODEL)
    ffn_w_up = nrm(ks[20], (DEPTH, D_MODEL, D_FF), D_MODEL)
    ffn_w_down = nrm(ks[21], (DEPTH, D_FF, D_MODEL), D_FF)
    return {
        "x": x,
        "ml_w_in": ml_w_in, "ml_b_if": ml_b_if, "ml_head_norm": ml_head_norm, "ml_w_out": ml_w_out,
        "lru_w_in": lru_w_in, "lru_conv_w": lru_conv_w, "lru_conv_b": lru_conv_b,
        "lru_w_gate_a": lru_w_gate_a, "lru_b_gate_a": lru_b_gate_a,
        "lru_w_gate_x": lru_w_gate_x, "lru_b_gate_x": lru_b_gate_x,
        "lru_lambda": lru_lambda, "lru_w_out": lru_w_out,
        "norm_pre_mix": norm_pre_mix, "norm_post_mix": norm_post_mix,
        "norm_pre_ffn": norm_pre_ffn, "norm_post_ffn": norm_post_ffn,
        "ffn_w_gate": ffn_w_gate, "ffn_w_up": ffn_w_up, "ffn_w_down": ffn_w_down,
    }


def reference(x, ml_w_in, ml_b_if, ml_head_norm, ml_w_out,
              lru_w_in, lru_conv_w, lru_conv_b, lru_w_gate_a, lru_b_gate_a,
              lru_w_gate_x, lru_b_gate_x, lru_lambda, lru_w_out,
              norm_pre_mix, norm_post_mix, norm_pre_ffn, norm_post_ffn,
              ffn_w_gate, ffn_w_up, ffn_w_down):
    h = x
    for layer in range(DEPTH):
        j = layer // N_MIXERS
        y = rms_norm(h, norm_pre_mix[layer])
        if layer % N_MIXERS == 0:
            y = mlstm_mixer(y, ml_w_in[j], ml_b_if[j], ml_head_norm[j], ml_w_out[j])
        else:
            y = rglru_mixer(y, lru_w_in[j], lru_conv_w[j], lru_conv_b[j],
                            lru_w_gate_a[j], lru_b_gate_a[j], lru_w_gate_x[j], lru_b_gate_x[j],
                            lru_lambda[j], lru_w_out[j])
        h = h + rms_norm(y, norm_post_mix[layer])
        y = swiglu(rms_norm(h, norm_pre_ffn[layer]), ffn_w_gate[layer], ffn_w_up[layer], ffn_w_down[layer])
        h = h + rms_norm(y, norm_post_ffn[layer])
    return h
```

```python
import functools

import jax
import jax.numpy as jnp
from jax import lax
from jax.experimental import pallas as pl
from jax.experimental.pallas import tpu as pltpu

D_MODEL = 1024
ML_HEADS = 8
ML_QK_DIM = 64
ML_V_DIM = 128
ML_QK = ML_HEADS * ML_QK_DIM
ML_V = ML_HEADS * ML_V_DIM
GATE_CAP = 15.0
ML_M_INIT = -1e30
LRU_WIDTH = D_MODEL
LRU_BLOCKS = 4
LRU_BLOCK = LRU_WIDTH // LRU_BLOCKS
CONV_WIDTH = 4
LRU_C = 8.0
D_FF = 2816
EPS = 1e-6

SUBLANES = 8
LANES = 128
VMEM_LIMIT_BYTES = 56 * 1024 * 1024

ML_CHUNK = 128
ML_TILE = 256
LRU_TILE = 256
FFN_TILE = 512
FFN_CHUNK = 256

F32 = jnp.float32
BF16 = jnp.bfloat16


def _rms(x, g):
    return x * lax.rsqrt(jnp.mean(x * x, axis=-1, keepdims=True) + EPS) * g


def _dot(a, b):
    return jnp.dot(a, b, preferred_element_type=F32)


def _dot_nt(a, b):
    return lax.dot_general(a, b, (((1,), (1,)), ((), ())), preferred_element_type=F32)


def _const_spec(shape):
    zeros = (0,) * len(shape)
    return pl.BlockSpec(shape, lambda *_: zeros, pipeline_mode=pl.Buffered(1))


def _ffn_kernel(h_ref, gpre_ref, gpost_ref, wgu_ref, wd_ref, o_ref, *, n_chunks, fc):
    h = h_ref[...]
    xn = _rms(h, gpre_ref[...]).astype(BF16)
    acc = None
    for c in range(n_chunks):
        gu = _dot(xn, wgu_ref[c])
        g = gu[:, :fc]
        a = (g * jax.nn.sigmoid(g) * gu[:, fc:]).astype(BF16)
        part = _dot(a, wd_ref[c])
        acc = part if acc is None else acc + part
    o_ref[...] = h + _rms(acc, gpost_ref[...])


def _ffn_layer(h, g_pre, g_post, w_gate, w_up, w_down):
    n_tok = h.shape[0]
    fc = FFN_CHUNK
    n_chunks = D_FF // fc
    tile = min(FFN_TILE, n_tok)
    wg = w_gate.astype(BF16).reshape(D_MODEL, n_chunks, fc)
    wu = w_up.astype(BF16).reshape(D_MODEL, n_chunks, fc)
    wgu = jnp.concatenate([wg, wu], axis=-1).transpose(1, 0, 2)
    wd = w_down.astype(BF16).reshape(n_chunks, fc, D_MODEL)
    tok_spec = pl.BlockSpec((tile, D_MODEL), lambda i: (i, 0))
    return pl.pallas_call(
        functools.partial(_ffn_kernel, n_chunks=n_chunks, fc=fc),
        grid=(n_tok // tile,),
        in_specs=[tok_spec, _const_spec((1, D_MODEL)), _const_spec((1, D_MODEL)),
                  _const_spec((n_chunks, D_MODEL, 2 * fc)), _const_spec((n_chunks, fc, D_MODEL))],
        out_specs=tok_spec,
        out_shape=jax.ShapeDtypeStruct((n_tok, D_MODEL), F32),
        compiler_params=pltpu.CompilerParams(
            dimension_semantics=("parallel",), vmem_limit_bytes=VMEM_LIMIT_BYTES),
        name="swiglu",
    )(h, g_pre.reshape(1, D_MODEL), g_post.reshape(1, D_MODEL), wgu, wd)


def _soft_cap(z):
    return GATE_CAP * jnp.tanh(z / GATE_CAP)


def _mlstm_kernel(h_ref, gpre_ref, gpost_ref, wmain_ref, wkt_ref, wg_ref, wgt_ref, bcol_ref,
                  brow_ref, hnorm_ref, wout_ref, o_ref, c_ref, m_ref, hs_ref, *, tile, chunk):
    H, dk, dv, L = ML_HEADS, ML_QK_DIM, ML_V_DIM, chunk

    @pl.when(pl.program_id(1) == 0)
    def _():
        c_ref[...] = jnp.zeros_like(c_ref)
        m_ref[...] = jnp.full_like(m_ref, ML_M_INIT)

    h = h_ref[...]
    xn = _rms(h, gpre_ref[...]).astype(BF16)
    proj = _dot(xn, wmain_ref[...])
    kt = _dot_nt(wkt_ref[...], xn)
    gcol = _soft_cap(_dot(xn, wg_ref[...]) + bcol_ref[...])
    grow = _soft_cap(_dot_nt(wgt_ref[...], xn) + brow_ref[...])
    lf_col = jax.nn.log_sigmoid(gcol)
    li_row = grow[0:H, :]
    lf_row = jax.nn.log_sigmoid(grow[H:2 * H, :])

    r_i = lax.broadcasted_iota(jnp.int32, (tile, tile), 0)
    c_i = lax.broadcasted_iota(jnp.int32, (tile, tile), 1)
    same_chunk = (r_i // L) == (c_i // L)
    lower = jnp.where(same_chunk & (c_i <= r_i), 1.0, 0.0).astype(F32)
    upper = jnp.where(same_chunk & (r_i <= c_i), 1.0, 0.0).astype(F32)
    b_col = jnp.dot(lower, lf_col, preferred_element_type=F32, precision=lax.Precision.HIGHEST)
    b_row = jnp.dot(lf_row, upper, preferred_element_type=F32, precision=lax.Precision.HIGHEST)

    causal = (lax.broadcasted_iota(jnp.int32, (L, L), 1)
              <= lax.broadcasted_iota(jnp.int32, (L, L), 0))
    ones_aug = jnp.ones((L, dv), BF16)

    for c in range(tile // L):
        r0 = c * L
        for hh in range(H):
            q_h = proj[r0:r0 + L, hh * dk:(hh + 1) * dk].astype(BF16)
            v_h = proj[r0:r0 + L, ML_QK + hh * dv:ML_QK + (hh + 1) * dv]
            o_h = proj[r0:r0 + L, ML_QK + ML_V + hh * dv:ML_QK + ML_V + (hh + 1) * dv]
            kt_h = kt[hh * dk:(hh + 1) * dk, r0:r0 + L]
            bc = b_col[r0:r0 + L, H + hh:H + hh + 1]
            br = b_row[hh:hh + 1, r0:r0 + L]
            lir = li_row[hh:hh + 1, r0:r0 + L]
            m_prev = m_ref[hh]
            c_prev = c_ref[hh]

            d_mat = jnp.where(causal, bc - br + lir, -jnp.inf)
            inter = bc + m_prev
            m_t = jnp.maximum(jnp.max(d_mat, axis=-1, keepdims=True), inter)
            s_mat = _dot(q_h, kt_h.astype(BF16))
            p_mat = jnp.exp(d_mat - m_t) * s_mat
            w_inter = jnp.exp(inter - m_t)

            g_tot = br[:, L - 1:L]
            a_row = g_tot - br + lir
            m_new = jnp.maximum(g_tot + m_prev, jnp.max(a_row, axis=-1, keepdims=True))
            decay = jnp.exp(g_tot + m_prev - m_new)
            wkt = kt_h * jnp.exp(a_row - m_new)

            v_aug = jnp.concatenate([v_h.astype(BF16), ones_aug], axis=1)
            lhs = jnp.concatenate([p_mat, wkt], axis=0).astype(BF16)
            res = _dot(lhs, v_aug)
            tot = res[0:L] + w_inter * _dot(q_h, c_prev.astype(BF16))
            den = jnp.maximum(jnp.abs(tot[:, dv:dv + 1]), jnp.exp(-m_t))
            h_out = tot[:, 0:dv] / den
            c_ref[hh] = decay * c_prev + res[L:L + dk]
            m_ref[hh] = m_new

            hn = h_out * lax.rsqrt(jnp.mean(h_out * h_out, axis=-1, keepdims=True) + EPS)
            hn = hn * hnorm_ref[:, hh * dv:(hh + 1) * dv]
            hs_ref[r0:r0 + L, hh * dv:(hh + 1) * dv] = (hn * jax.nn.sigmoid(o_h)).astype(BF16)

    y = _dot(hs_ref[...], wout_ref[...])
    o_ref[...] = h + _rms(y, gpost_ref[...])


def _mlstm_layer(h, g_pre, g_post, w_in, b_if, head_norm, w_out):
    B, S, _ = h.shape
    H = ML_HEADS
    tile = min(ML_TILE, S)
    chunk = min(ML_CHUNK, tile)
    w_q = w_in[:, :ML_QK] * (ML_QK_DIM ** -0.5)
    w_k = w_in[:, ML_QK:2 * ML_QK]
    w_vo = w_in[:, 2 * ML_QK:2 * ML_QK + 2 * ML_V]
    w_g = w_in[:, 2 * ML_QK + 2 * ML_V:]
    w_main = jnp.concatenate([w_q, w_vo], axis=1).astype(BF16)
    w_kt = w_k.T.astype(BF16)
    w_gcol = jnp.pad(w_g, ((0, 0), (0, LANES - 2 * H))).astype(BF16)
    w_grow = w_g.T.astype(BF16)
    b_col = jnp.pad(b_if.astype(F32), (0, LANES - 2 * H)).reshape(1, LANES)
    b_row = b_if.astype(F32).reshape(2 * H, 1)
    tok_spec = pl.BlockSpec((None, tile, D_MODEL), lambda b, t: (b, t, 0))
    n_main = ML_QK + 2 * ML_V
    return pl.pallas_call(
        functools.partial(_mlstm_kernel, tile=tile, chunk=chunk),
        grid=(B, S // tile),
        in_specs=[tok_spec, _const_spec((1, D_MODEL)), _const_spec((1, D_MODEL)),
                  _const_spec((D_MODEL, n_main)), _const_spec((ML_QK, D_MODEL)),
                  _const_spec((D_MODEL, LANES)), _const_spec((2 * H, D_MODEL)),
                  _const_spec((1, LANES)), _const_spec((2 * H, 1)),
                  _const_spec((1, ML_V)), _const_spec((ML_V, D_MODEL))],
        out_specs=tok_spec,
        out_shape=jax.ShapeDtypeStruct((B, S, D_MODEL), F32),
        scratch_shapes=[pltpu.VMEM((H, ML_QK_DIM, 2 * ML_V_DIM), F32),
                        pltpu.VMEM((H, 1, 1), F32),
                        pltpu.VMEM((tile, ML_V), BF16)],
        compiler_params=pltpu.CompilerParams(
            dimension_semantics=("parallel", "arbitrary"), vmem_limit_bytes=VMEM_LIMIT_BYTES),
        name="mlstm",
    )(h, g_pre.reshape(1, D_MODEL), g_post.reshape(1, D_MODEL), w_main, w_kt, w_gcol, w_grow,
      b_col, b_row, head_norm.reshape(1, ML_V), w_out.astype(BF16))


def _lru_kernel(h_ref, gpre_ref, gpost_ref, win_ref, convw_ref, convb_ref, wgate_ref, ba_ref,
                bx_ref, lam_ref, wout_ref, o_ref, uext_ref, hc_ref, *, tile):
    W = LRU_WIDTH
    pad = SUBLANES

    @pl.when(pl.program_id(1) == 0)
    def _():
        uext_ref[0:pad, :] = jnp.zeros((pad, W), F32)
        hc_ref[...] = jnp.zeros_like(hc_ref)

    h = h_ref[...]
    xn = _rms(h, gpre_ref[...]).astype(BF16)
    proj = _dot(xn, win_ref[...])
    gate_branch = proj[:, :W]
    u = proj[:, W:]

    uext_ref[pad:pad + tile, :] = u
    uc = convw_ref[CONV_WIDTH - 1:CONV_WIDTH, :] * u + convb_ref[...]
    for j in range(1, CONV_WIDTH):
        uc = uc + convw_ref[CONV_WIDTH - 1 - j:CONV_WIDTH - j, :] * uext_ref[pad - j:pad - j + tile, :]
    uext_ref[0:pad, :] = u[tile - pad:tile, :]

    ucb = uc.astype(BF16)
    gates = [_dot(ucb[:, n * LRU_BLOCK:(n + 1) * LRU_BLOCK], wgate_ref[n]) for n in range(LRU_BLOCKS)]
    ga = jnp.concatenate([g[:, :LRU_BLOCK] for g in gates], axis=1)
    gx = jnp.concatenate([g[:, LRU_BLOCK:] for g in gates], axis=1)
    r = jax.nn.sigmoid(ga + ba_ref[...])
    i = jax.nn.sigmoid(gx + bx_ref[...])
    log_a = LRU_C * r * jax.nn.log_sigmoid(lam_ref[...])
    a = jnp.exp(log_a)
    b = jnp.sqrt(-jnp.tanh(log_a) * (1.0 + a * a)) * i * uc

    n_grp = tile // SUBLANES
    a3 = a.reshape(n_grp, SUBLANES, W)
    b3 = b.reshape(n_grp, SUBLANES, W)
    row = lax.broadcasted_iota(jnp.int32, (n_grp, SUBLANES, W), 1)
    d = 1
    while d < SUBLANES:
        a_sh = jnp.where(row < d, 1.0, pltpu.roll(a3, d, axis=1))
        b_sh = jnp.where(row < d, 0.0, pltpu.roll(b3, d, axis=1))
        b3 = a3 * b_sh + b3
        a3 = a3 * a_sh
        d *= 2
    carry = hc_ref[...]
    groups = []
    for g in range(n_grp):
        hg = a3[g] * carry + b3[g]
        carry = hg[SUBLANES - 1:SUBLANES, :]
        groups.append(hg)
    hc_ref[...] = carry
    hseq = jnp.concatenate(groups, axis=0)

    y = (hseq * jax.nn.gelu(gate_branch)).astype(BF16)
    o_ref[...] = h + _rms(_dot(y, wout_ref[...]), gpost_ref[...])


def _lru_layer(h, g_pre, g_post, w_in, conv_w, conv_b, w_gate_a, b_gate_a, w_gate_x, b_gate_x,
               lam, w_out):
    B, S, _ = h.shape
    W = LRU_WIDTH
    tile = min(LRU_TILE, S)
    w_gate = jnp.concatenate([w_gate_a, w_gate_x], axis=-1).astype(BF16)
    tok_spec = pl.BlockSpec((None, tile, D_MODEL), lambda b, t: (b, t, 0))
    row = lambda v: v.astype(F32).reshape(1, W)
    return pl.pallas_call(
        functools.partial(_lru_kernel, tile=tile),
        grid=(B, S // tile),
        in_specs=[tok_spec, _const_spec((1, D_MODEL)), _const_spec((1, D_MODEL)),
                  _const_spec((D_MODEL, 2 * W)), _const_spec((CONV_WIDTH, W)), _const_spec((1, W)),
                  _const_spec((LRU_BLOCKS, LRU_BLOCK, 2 * LRU_BLOCK)), _const_spec((1, W)),
                  _const_spec((1, W)), _const_spec((1, W)), _const_spec((W, D_MODEL))],
        out_specs=tok_spec,
        out_shape=jax.ShapeDtypeStruct((B, S, D_MODEL), F32),
        scratch_shapes=[pltpu.VMEM((tile + SUBLANES, W), F32), pltpu.VMEM((1, W), F32)],
        compiler_params=pltpu.CompilerParams(
            dimension_semantics=("parallel", "arbitrary"), vmem_limit_bytes=VMEM_LIMIT_BYTES),
        name="rglru",
    )(h, g_pre.reshape(1, D_MODEL), g_post.reshape(1, D_MODEL), w_in.astype(BF16),
      conv_w.astype(F32), row(conv_b), w_gate, row(b_gate_a), row(b_gate_x), row(lam),
      w_out.astype(BF16))


def kernel(x, ml_w_in, ml_b_if, ml_head_norm, ml_w_out, lru_w_in, lru_conv_w, lru_conv_b, lru_w_gate_a, lru_b_gate_a, lru_w_gate_x, lru_b_gate_x, lru_lambda, lru_w_out, norm_pre_mix, norm_post_mix, norm_pre_ffn, norm_post_ffn, ffn_w_gate, ffn_w_up, ffn_w_down):
    B, S, D = x.shape
    depth = norm_pre_mix.shape[0]
    h = x
    for layer in range(depth):
        j = layer // 2
        if layer % 2 == 0:
            h = _mlstm_layer(h, norm_pre_mix[layer], norm_post_mix[layer], ml_w_in[j], ml_b_if[j],
                             ml_head_norm[j], ml_w_out[j])
        else:
            h = _lru_layer(h, norm_pre_mix[layer], norm_post_mix[layer], lru_w_in[j], lru_conv_w[j],
                           lru_conv_b[j], lru_w_gate_a[j], lru_b_gate_a[j], lru_w_gate_x[j],
                           lru_b_gate_x[j], lru_lambda[j], lru_w_out[j])
        h = _ffn_layer(h.reshape(B * S, D), norm_pre_ffn[layer], norm_post_ffn[layer],
                       ffn_w_gate[layer], ffn_w_up[layer], ffn_w_down[layer]).reshape(B, S, D)
    return h
```

```python
import functools
import math

import jax
import jax.numpy as jnp
from jax import lax
from jax.experimental import pallas as pl
from jax.experimental.pallas import tpu as pltpu

D_MODEL = 1024
ML_HEADS = 8
ML_QK_DIM = 64
ML_V_DIM = 128
ML_QK = ML_HEADS * ML_QK_DIM
ML_V = ML_HEADS * ML_V_DIM
GATE_CAP = 15.0
ML_M_INIT = -1e30
LRU_WIDTH = D_MODEL
LRU_BLOCKS = 4
LRU_BLOCK = LRU_WIDTH // LRU_BLOCKS
CONV_WIDTH = 4
LRU_C = 8.0
D_FF = 2816
EPS = 1e-6
LOG2E = math.log2(math.e)
GELU_K0 = math.sqrt(2.0 / math.pi)
GELU_K1 = GELU_K0 * 0.044715

SUBLANES = 8
LANES = 128
VMEM_LIMIT_BYTES = 56 * 1024 * 1024

ML_CHUNK = 128
ML_TILE = 512
LRU_ROWS = 512
FFN_TILE = 512
FFN_CHUNK = 256

F32 = jnp.float32
BF16 = jnp.bfloat16
BATCH_MAJOR = "batch_major"
TIME_MAJOR = "time_major"


def _rms(x, g):
    return x * lax.rsqrt(jnp.mean(x * x, axis=-1, keepdims=True) + EPS) * g


def _sigmoid(x):
    return 0.5 * jnp.tanh(0.5 * x) + 0.5


def _dot(a, b):
    return jnp.dot(a, b, preferred_element_type=F32)


def _dot_nt(a, b):
    return lax.dot_general(a, b, (((1,), (1,)), ((), ())), preferred_element_type=F32)


def _const_spec(shape):
    zeros = (0,) * len(shape)
    return pl.BlockSpec(shape, lambda *_: zeros, pipeline_mode=pl.Buffered(1))


def _token_array(layout, B, S):
    shape = (B, S, D_MODEL) if layout == BATCH_MAJOR else (S, B * D_MODEL)
    return jax.ShapeDtypeStruct(shape, F32)


def _token_spec(layout, tile):
    if layout == BATCH_MAJOR:
        return pl.BlockSpec((None, tile, D_MODEL), lambda b, t: (b, t, 0))
    return pl.BlockSpec((tile, D_MODEL), lambda b, t: (t, b))


def _ffn_kernel(h_ref, gpre_ref, gpost_ref, wgu_ref, wd_ref, o_ref, *, n_chunks, fc):
    h = h_ref[...]
    xn = _rms(h, gpre_ref[...]).astype(BF16)
    acc = None
    for c in range(n_chunks):
        gu = _dot(xn, wgu_ref[c])
        g = gu[:, :fc]
        a = (g * _sigmoid(g) * gu[:, fc:]).astype(BF16)
        part = _dot(a, wd_ref[c])
        acc = part if acc is None else acc + part
    o_ref[...] = h + _rms(acc, gpost_ref[...])


def _ffn_layer(h, B, S, in_layout, out_layout, g_pre, g_post, w_gate, w_up, w_down):
    fc = FFN_CHUNK
    n_chunks = D_FF // fc
    tile = min(FFN_TILE, S)
    wg = w_gate.astype(BF16).reshape(D_MODEL, n_chunks, fc)
    wu = w_up.astype(BF16).reshape(D_MODEL, n_chunks, fc)
    wgu = jnp.concatenate([wg, wu], axis=-1).transpose(1, 0, 2)
    wd = w_down.astype(BF16).reshape(n_chunks, fc, D_MODEL)
    if in_layout == out_layout:
        h = h.reshape(B * S, D_MODEL)
        grid = (B * S // tile,)
        in_spec = out_spec = pl.BlockSpec((tile, D_MODEL), lambda i: (i, 0))
        out_shape = jax.ShapeDtypeStruct((B * S, D_MODEL), F32)
    else:
        grid = (B, S // tile)
        in_spec, out_spec = _token_spec(in_layout, tile), _token_spec(out_layout, tile)
        out_shape = _token_array(out_layout, B, S)
    out = pl.pallas_call(
        functools.partial(_ffn_kernel, n_chunks=n_chunks, fc=fc),
        grid=grid,
        in_specs=[in_spec, _const_spec((1, D_MODEL)), _const_spec((1, D_MODEL)),
                  _const_spec((n_chunks, D_MODEL, 2 * fc)), _const_spec((n_chunks, fc, D_MODEL))],
        out_specs=out_spec,
        out_shape=out_shape,
        compiler_params=pltpu.CompilerParams(
            dimension_semantics=("parallel",) * len(grid), vmem_limit_bytes=VMEM_LIMIT_BYTES),
        name="swiglu",
    )(h, g_pre.reshape(1, D_MODEL), g_post.reshape(1, D_MODEL), wgu, wd)
    return out.reshape(_token_array(out_layout, B, S).shape)


def _soft_cap(z):
    return GATE_CAP * jnp.tanh(z / GATE_CAP)


def _lane_prefix(x, combine, identity):
    n = x.shape[-1]
    lane = lax.broadcasted_iota(jnp.int32, x.shape, x.ndim - 1)
    d = 1
    while d < n:
        x = combine(x, jnp.where(lane >= d, pltpu.roll(x, d, axis=x.ndim - 1), identity))
        d *= 2
    return x


def _mlstm_kernel(h_ref, gpre_ref, gpost_ref, wmain_ref, wkt_ref, wgt_ref, brow_ref, hnorm_ref,
                  wout_ref, o_ref, c_ref, m_ref, hs_ref, *, tile, chunk):
    H, dk, dv, L = ML_HEADS, ML_QK_DIM, ML_V_DIM, chunk

    @pl.when(pl.program_id(1) == 0)
    def _():
        c_ref[...] = jnp.zeros_like(c_ref)
        m_ref[...] = jnp.full_like(m_ref, ML_M_INIT)

    h = h_ref[...]
    xn = _rms(h, gpre_ref[...]).astype(BF16)
    grow = _soft_cap(_dot_nt(wgt_ref[...], xn) + brow_ref[...])
    li_all = grow[0:H, :]
    lf_all = jax.nn.log_sigmoid(grow[H:2 * H, :])
    proj = _dot(xn, wmain_ref[...])
    kt = _dot_nt(wkt_ref[...], xn)

    causal = (lax.broadcasted_iota(jnp.int32, (L, L), 1)
              <= lax.broadcasted_iota(jnp.int32, (L, L), 0))
    ones_aug = jnp.ones((L, dv), BF16)
    zeros_kk = jnp.zeros((dk, dk), F32)

    m_prev = m_ref[...]
    for c in range(tile // L):
        r0 = c * L
        li = li_all[:, r0:r0 + L]
        b = _lane_prefix(lf_all[:, r0:r0 + L], jnp.add, 0.0)
        rr = li - b
        rr_max = _lane_prefix(rr, jnp.maximum, -jnp.inf)
        g_tot = b[:, L - 1:L]
        m_t = b + jnp.maximum(rr_max, m_prev)
        m_new = jnp.maximum(g_tot + m_prev, g_tot + rr_max[:, L - 1:L])
        decay = jnp.exp(g_tot + m_prev - m_new)
        ea = jnp.exp(g_tot + rr - m_new)
        rr2 = rr * LOG2E
        per_row = jnp.concatenate(
            [(b - m_t) * LOG2E, jnp.exp(b + m_prev - m_t), jnp.exp(-m_t),
             jnp.zeros((L - 3 * H, L), F32)], axis=0)
        per_row = per_row.T
        m_prev = m_new

        for hh in range(H):
            q_h = proj[r0:r0 + L, hh * dk:(hh + 1) * dk]
            v_h = proj[r0:r0 + L, ML_QK + hh * dv:ML_QK + (hh + 1) * dv]
            o_h = proj[r0:r0 + L, ML_QK + ML_V + hh * dv:ML_QK + ML_V + (hh + 1) * dv]
            kt_h = kt[hh * dk:(hh + 1) * dk, r0:r0 + L]
            cm2 = per_row[:, hh:hh + 1]
            w_inter = per_row[:, H + hh:H + hh + 1]
            emt = per_row[:, 2 * H + hh:2 * H + hh + 1]
            c_prev = c_ref[hh]

            s_mat = _dot(q_h.astype(BF16), kt_h.astype(BF16))
            p_mat = jnp.where(causal, jnp.exp2(cm2 + rr2[hh:hh + 1, :]) * s_mat, 0.0)
            lhs = jnp.concatenate(
                [jnp.concatenate([p_mat, q_h * w_inter], axis=1),
                 jnp.concatenate([kt_h * ea[hh:hh + 1, :], zeros_kk], axis=1)], axis=0).astype(BF16)
            rhs = jnp.concatenate(
                [jnp.concatenate([v_h.astype(BF16), ones_aug], axis=1), c_prev.astype(BF16)], axis=0)
            res = _dot(lhs, rhs)
            c_ref[hh] = decay[hh:hh + 1, :] * c_prev + res[L:L + dk]

            den = jnp.maximum(jnp.abs(res[0:L, dv:2 * dv]), emt)
            h_out = res[0:L, 0:dv] / den
            hn = h_out * lax.rsqrt(jnp.mean(h_out * h_out, axis=-1, keepdims=True) + EPS)
            hn = hn * hnorm_ref[:, hh * dv:(hh + 1) * dv]
            hs_ref[r0:r0 + L, hh * dv:(hh + 1) * dv] = (hn * _sigmoid(o_h)).astype(BF16)
    m_ref[...] = m_prev

    y = _dot(hs_ref[...], wout_ref[...])
    o_ref[...] = h + _rms(y, gpost_ref[...])


def _mlstm_layer(h, B, S, out_layout, g_pre, g_post, w_in, b_if, head_norm, w_out):
    H = ML_HEADS
    tile = min(ML_TILE, S)
    chunk = min(ML_CHUNK, tile)
    w_q = w_in[:, :ML_QK] * (ML_QK_DIM ** -0.5)
    w_k = w_in[:, ML_QK:2 * ML_QK]
    w_vo = w_in[:, 2 * ML_QK:2 * ML_QK + 2 * ML_V]
    w_g = w_in[:, 2 * ML_QK + 2 * ML_V:]
    w_main = jnp.concatenate([w_q, w_vo], axis=1).astype(BF16)
    w_kt = w_k.T.astype(BF16)
    w_grow = w_g.T.astype(BF16)
    b_row = b_if.astype(F32).reshape(2 * H, 1)
    n_main = ML_QK + 2 * ML_V
    return pl.pallas_call(
        functools.partial(_mlstm_kernel, tile=tile, chunk=chunk),
        grid=(B, S // tile),
        in_specs=[_token_spec(BATCH_MAJOR, tile), _const_spec((1, D_MODEL)), _const_spec((1, D_MODEL)),
                  _const_spec((D_MODEL, n_main)), _const_spec((ML_QK, D_MODEL)),
                  _const_spec((2 * H, D_MODEL)), _const_spec((2 * H, 1)),
                  _const_spec((1, ML_V)), _const_spec((ML_V, D_MODEL))],
        out_specs=_token_spec(out_layout, tile),
        out_shape=_token_array(out_layout, B, S),
        scratch_shapes=[pltpu.VMEM((H, ML_QK_DIM, 2 * ML_V_DIM), F32),
                        pltpu.VMEM((H, 1), F32),
                        pltpu.VMEM((tile, ML_V), BF16)],
        compiler_params=pltpu.CompilerParams(
            dimension_semantics=("parallel", "arbitrary"), vmem_limit_bytes=VMEM_LIMIT_BYTES),
        name="mlstm",
    )(h, g_pre.reshape(1, D_MODEL), g_post.reshape(1, D_MODEL), w_main, w_kt, w_grow,
      b_row, head_norm.reshape(1, ML_V), w_out.astype(BF16))


def _lru_kernel(h_ref, gpre_ref, gpost_ref, win_ref, convw_ref, convb_ref, wgate_ref, ba_ref,
                bx_ref, lam_ref, wout_ref, o_ref, uext_ref, hc_ref, *, rows, batch):
    W = LRU_WIDTH
    halo = (CONV_WIDTH - 1) * batch

    @pl.when(pl.program_id(0) == 0)
    def _():
        uext_ref[0:halo, :] = jnp.zeros((halo, W), F32)
        hc_ref[...] = jnp.zeros_like(hc_ref)

    h = h_ref[...]
    xn = _rms(h, gpre_ref[...]).astype(BF16)
    proj = _dot(xn, win_ref[...])
    gate_branch = proj[:, :W]
    u = proj[:, W:]

    uext_ref[halo:halo + rows, :] = u
    uc = convw_ref[CONV_WIDTH - 1:CONV_WIDTH, :] * u + convb_ref[...]
    for j in range(1, CONV_WIDTH):
        start = halo - j * batch
        uc = uc + convw_ref[CONV_WIDTH - 1 - j:CONV_WIDTH - j, :] * uext_ref[start:start + rows, :]
    uext_ref[0:halo, :] = u[rows - halo:rows, :]

    ucb = uc.astype(BF16)
    gates = [_dot(ucb[:, n * LRU_BLOCK:(n + 1) * LRU_BLOCK], wgate_ref[n]) for n in range(LRU_BLOCKS)]
    ga = jnp.concatenate([g[:, :LRU_BLOCK] for g in gates], axis=1)
    gx = jnp.concatenate([g[:, LRU_BLOCK:] for g in gates], axis=1)
    tr = jnp.tanh(ga + ba_ref[...])
    ti = jnp.tanh(gx + bx_ref[...])
    half_c = (-0.5 * LRU_C) * jax.nn.log_sigmoid(lam_ref[...])
    neg_log_a = tr * half_c + half_c
    a = jnp.exp2(neg_log_a * -LOG2E)
    gain2 = jnp.tanh(neg_log_a) * (a * a + 1.0)
    gain = gain2 * lax.rsqrt(jnp.maximum(gain2, jnp.finfo(F32).tiny))
    b2 = gain * uc * (ti + 1.0)

    steps = rows // batch
    carry = hc_ref[...]
    hs = []
    for t in range(steps):
        carry = a[t * batch:(t + 1) * batch, :] * carry + b2[t * batch:(t + 1) * batch, :]
        hs.append(carry)
    hc_ref[...] = carry
    hseq2 = jnp.concatenate(hs, axis=0)

    x = gate_branch
    inner = x * (GELU_K1 * (x * x) + GELU_K0)
    y = (hseq2 * (0.25 * x) * (jnp.tanh(inner) + 1.0)).astype(BF16)
    o_ref[...] = h + _rms(_dot(y, wout_ref[...]), gpost_ref[...])


def _lru_layer(h, B, S, g_pre, g_post, w_in, conv_w, conv_b, w_gate_a, b_gate_a, w_gate_x,
               b_gate_x, lam, w_out):
    W = LRU_WIDTH
    assert B % SUBLANES == 0, "time-major RG-LRU wants whole sublane groups per time step"
    rows = min(LRU_ROWS, S * B)
    halo = (CONV_WIDTH - 1) * B
    w_gate = (0.5 * jnp.concatenate([w_gate_a, w_gate_x], axis=-1)).astype(BF16)
    tok_spec = pl.BlockSpec((rows, D_MODEL), lambda i: (i, 0))
    row = lambda v: v.astype(F32).reshape(1, W)
    out = pl.pallas_call(
        functools.partial(_lru_kernel, rows=rows, batch=B),
        grid=(S * B // rows,),
        in_specs=[tok_spec, _const_spec((1, D_MODEL)), _const_spec((1, D_MODEL)),
                  _const_spec((D_MODEL, 2 * W)), _const_spec((CONV_WIDTH, W)), _const_spec((1, W)),
                  _const_spec((LRU_BLOCKS, LRU_BLOCK, 2 * LRU_BLOCK)), _const_spec((1, W)),
                  _const_spec((1, W)), _const_spec((1, W)), _const_spec((W, D_MODEL))],
        out_specs=tok_spec,
        out_shape=jax.ShapeDtypeStruct((S * B, D_MODEL), F32),
        scratch_shapes=[pltpu.VMEM((rows + halo, W), F32), pltpu.VMEM((B, W), F32)],
        compiler_params=pltpu.CompilerParams(
            dimension_semantics=("arbitrary",), vmem_limit_bytes=VMEM_LIMIT_BYTES),
        name="rglru",
    )(h.reshape(S * B, D_MODEL), g_pre.reshape(1, D_MODEL), g_post.reshape(1, D_MODEL),
      w_in.astype(BF16), conv_w.astype(F32), row(conv_b), w_gate, row(0.5 * b_gate_a), row(0.5 * b_gate_x),
      row(lam), w_out.astype(BF16))
    return out.reshape(S, B * D_MODEL)


def kernel(x, ml_w_in, ml_b_if, ml_head_norm, ml_w_out, lru_w_in, lru_conv_w, lru_conv_b, lru_w_gate_a, lru_b_gate_a, lru_w_gate_x, lru_b_gate_x, lru_lambda, lru_w_out, norm_pre_mix, norm_post_mix, norm_pre_ffn, norm_post_ffn, ffn_w_gate, ffn_w_up, ffn_w_down):
    B, S, _ = x.shape
    depth = norm_pre_mix.shape[0]

    def mixer_layout(layer):
        return TIME_MAJOR if layer < depth and layer % 2 == 1 else BATCH_MAJOR

    h = x
    for layer in range(depth):
        j = layer // 2
        next_layout = mixer_layout(layer + 1)
        if layer % 2 == 0:
            h = _mlstm_layer(h, B, S, next_layout, norm_pre_mix[layer], norm_post_mix[layer],
                             ml_w_in[j], ml_b_if[j], ml_head_norm[j], ml_w_out[j])
            ffn_in = next_layout
        else:
            h = _lru_layer(h, B, S, norm_pre_mix[layer], norm_post_mix[layer], lru_w_in[j],
                           lru_conv_w[j], lru_conv_b[j], lru_w_gate_a[j], lru_b_gate_a[j],
                           lru_w_gate_x[j], lru_b_gate_x[j], lru_lambda[j], lru_w_out[j])
            ffn_in = TIME_MAJOR
        h = _ffn_layer(h, B, S, ffn_in, next_layout, norm_pre_ffn[layer], norm_post_ffn[layer],
                       ffn_w_gate[layer], ffn_w_up[layer], ffn_w_down[layer])
    return h
```

```python
import functools
import math

import jax
import jax.numpy as jnp
from jax import lax
from jax.experimental import pallas as pl
from jax.experimental.pallas import tpu as pltpu

D_MODEL = 1024
ML_HEADS = 8
ML_QK_DIM = 64
ML_V_DIM = 128
ML_QK = ML_HEADS * ML_QK_DIM
ML_V = ML_HEADS * ML_V_DIM
GATE_CAP = 15.0
ML_M_INIT = -1e30
LRU_WIDTH = D_MODEL
LRU_BLOCKS = 4
LRU_BLOCK = LRU_WIDTH // LRU_BLOCKS
CONV_WIDTH = 4
LRU_C = 8.0
D_FF = 2816
EPS = 1e-6
LOG2E = math.log2(math.e)
GELU_K0 = math.sqrt(2.0 / math.pi)
GELU_K1 = GELU_K0 * 0.044715

SUBLANES = 8
LANES = 128
VMEM_LIMIT_BYTES = 56 * 1024 * 1024

ML_CHUNK = 128
ML_TILE = 512
LRU_ROWS = 512
FFN_TILE = 512
FFN_CHUNK = 256

F32 = jnp.float32
BF16 = jnp.bfloat16


def _rms(x, g):
    return x * lax.rsqrt(jnp.mean(x * x, axis=-1, keepdims=True) + EPS) * g


def _sigmoid(x):
    return 0.5 * jnp.tanh(0.5 * x) + 0.5


def _dot(a, b):
    return jnp.dot(a, b, preferred_element_type=F32)


def _dot_nt(a, b):
    return lax.dot_general(a, b, (((1,), (1,)), ((), ())), preferred_element_type=F32)


def _const_spec(shape):
    zeros = (0,) * len(shape)
    return pl.BlockSpec(shape, lambda *_: zeros, pipeline_mode=pl.Buffered(1))


def _ffn_kernel(h_ref, gpre_ref, gpost_ref, wg_ref, wu_ref, wd_ref, o_ref, *, fc):
    h = h_ref[...]
    xn = _rms(h, gpre_ref[...])
    acc = None
    for c in range(D_FF // fc):
        cols = slice(c * fc, (c + 1) * fc)
        half_g = 0.5 * _dot(xn, wg_ref[:, cols])
        a = (jnp.tanh(half_g) * half_g + half_g) * _dot(xn, wu_ref[:, cols])
        part = _dot(a, wd_ref[cols, :])
        acc = part if acc is None else acc + part
    o_ref[...] = h + _rms(acc, gpost_ref[...])


def _ffn_layer(h, g_pre, g_post, w_gate, w_up, w_down):
    n_tok = h.shape[0]
    tile = min(FFN_TILE, n_tok)
    tok_spec = pl.BlockSpec((tile, D_MODEL), lambda i: (i, 0))
    return pl.pallas_call(
        functools.partial(_ffn_kernel, fc=FFN_CHUNK),
        grid=(n_tok // tile,),
        in_specs=[tok_spec, _const_spec((1, D_MODEL)), _const_spec((1, D_MODEL)),
                  _const_spec((D_MODEL, D_FF)), _const_spec((D_MODEL, D_FF)),
                  _const_spec((D_FF, D_MODEL))],
        out_specs=tok_spec,
        out_shape=jax.ShapeDtypeStruct((n_tok, D_MODEL), F32),
        compiler_params=pltpu.CompilerParams(
            dimension_semantics=("parallel",), vmem_limit_bytes=VMEM_LIMIT_BYTES),
        name="swiglu",
    )(h, g_pre.reshape(1, D_MODEL), g_post.reshape(1, D_MODEL), w_gate, w_up, w_down)


def _soft_cap(z):
    return GATE_CAP * jnp.tanh(z / GATE_CAP)


def _lane_prefix(x, combine, identity):
    n = x.shape[-1]
    lane = lax.broadcasted_iota(jnp.int32, x.shape, x.ndim - 1)
    d = 1
    while d < n:
        x = combine(x, jnp.where(lane >= d, pltpu.roll(x, d, axis=x.ndim - 1), identity))
        d *= 2
    return x


def _mlstm_kernel(h_ref, gpre_ref, gpost_ref, win_ref, wkt_ref, wgt_ref, brow_ref, hnorm_ref,
                  wout_ref, o_ref, c_ref, m_ref, hs_ref, *, tile, chunk):
    H, dk, dv, L = ML_HEADS, ML_QK_DIM, ML_V_DIM, chunk
    log2_q_scale = -0.5 * math.log2(dk)

    @pl.when(pl.program_id(1) == 0)
    def _():
        c_ref[...] = jnp.zeros_like(c_ref)
        m_ref[...] = jnp.full_like(m_ref, ML_M_INIT)

    h = h_ref[...]
    xn = _rms(h, gpre_ref[...])
    grow = _soft_cap(_dot_nt(wgt_ref[...], xn) + brow_ref[...])
    li_all = grow[0:H, :]
    lf_all = jax.nn.log_sigmoid(grow[H:2 * H, :])
    q_all = _dot(xn, win_ref[:, 0:ML_QK])
    vo = _dot(xn, win_ref[:, 2 * ML_QK:2 * ML_QK + 2 * ML_V])
    kt = _dot_nt(wkt_ref[...], xn)

    causal = (lax.broadcasted_iota(jnp.int32, (L, L), 1)
              <= lax.broadcasted_iota(jnp.int32, (L, L), 0))
    ones_aug = jnp.ones((L, dv), F32)
    zeros_kk = jnp.zeros((dk, dk), F32)

    m_prev = m_ref[...]
    for c in range(tile // L):
        r0 = c * L
        li = li_all[:, r0:r0 + L]
        b = _lane_prefix(lf_all[:, r0:r0 + L], jnp.add, 0.0)
        rr = li - b
        rr_max = _lane_prefix(rr, jnp.maximum, -jnp.inf)
        g_tot = b[:, L - 1:L]
        m_t = b + jnp.maximum(rr_max, m_prev)
        m_new = jnp.maximum(g_tot + m_prev, g_tot + rr_max[:, L - 1:L])
        decay = jnp.exp(g_tot + m_prev - m_new)
        ea = jnp.exp(g_tot + rr - m_new)
        rr2 = rr * LOG2E
        per_row = jnp.concatenate(
            [(b - m_t) * LOG2E + log2_q_scale, jnp.exp(b + m_prev - m_t) * 2.0 ** log2_q_scale,
             jnp.exp(-m_t), jnp.zeros((L - 3 * H, L), F32)], axis=0)
        per_row = per_row.T
        m_prev = m_new

        for hh in range(H):
            q_h = q_all[r0:r0 + L, hh * dk:(hh + 1) * dk]
            v_h = vo[r0:r0 + L, hh * dv:(hh + 1) * dv]
            o_h = vo[r0:r0 + L, ML_V + hh * dv:ML_V + (hh + 1) * dv]
            kt_h = kt[hh * dk:(hh + 1) * dk, r0:r0 + L]
            cm2 = per_row[:, hh:hh + 1]
            w_inter = per_row[:, H + hh:H + hh + 1]
            emt = per_row[:, 2 * H + hh:2 * H + hh + 1]
            c_prev = c_ref[hh]

            s_mat = _dot(q_h, kt_h)
            p_mat = jnp.where(causal, jnp.exp2(cm2 + rr2[hh:hh + 1, :]) * s_mat, 0.0)
            lhs = jnp.concatenate(
                [jnp.concatenate([p_mat, q_h * w_inter], axis=1),
                 jnp.concatenate([kt_h * ea[hh:hh + 1, :], zeros_kk], axis=1)], axis=0)
            rhs = jnp.concatenate([jnp.concatenate([v_h, ones_aug], axis=1), c_prev], axis=0)
            res = _dot(lhs, rhs)
            c_ref[hh] = decay[hh:hh + 1, :] * c_prev + res[L:L + dk]

            den = jnp.maximum(jnp.abs(res[0:L, dv:2 * dv]), emt)
            h_out = res[0:L, 0:dv] / den
            hn = h_out * lax.rsqrt(jnp.mean(h_out * h_out, axis=-1, keepdims=True) + EPS)
            hn = hn * hnorm_ref[:, hh * dv:(hh + 1) * dv]
            hs_ref[r0:r0 + L, hh * dv:(hh + 1) * dv] = hn * _sigmoid(o_h)
    m_ref[...] = m_prev

    y = _dot(hs_ref[...], wout_ref[...])
    o_ref[...] = h + _rms(y, gpost_ref[...])


def _mlstm_layer(h, g_pre, g_post, w_in, b_if, head_norm, w_out):
    B, S, _ = h.shape
    H = ML_HEADS
    tile = min(ML_TILE, S)
    chunk = min(ML_CHUNK, tile)
    w_kt = w_in[:, ML_QK:2 * ML_QK].T
    w_grow = w_in[:, 2 * ML_QK + 2 * ML_V:].T
    b_row = b_if.astype(F32).reshape(2 * H, 1)
    tok_spec = pl.BlockSpec((None, tile, D_MODEL), lambda b, t: (b, t, 0))
    return pl.pallas_call(
        functools.partial(_mlstm_kernel, tile=tile, chunk=chunk),
        grid=(B, S // tile),
        in_specs=[tok_spec, _const_spec((1, D_MODEL)), _const_spec((1, D_MODEL)),
                  _const_spec(w_in.shape), _const_spec((ML_QK, D_MODEL)),
                  _const_spec((2 * H, D_MODEL)), _const_spec((2 * H, 1)),
                  _const_spec((1, ML_V)), _const_spec((ML_V, D_MODEL))],
        out_specs=tok_spec,
        out_shape=jax.ShapeDtypeStruct((B, S, D_MODEL), F32),
        scratch_shapes=[pltpu.VMEM((H, ML_QK_DIM, 2 * ML_V_DIM), F32),
                        pltpu.VMEM((H, 1), F32),
                        pltpu.VMEM((tile, ML_V), F32)],
        compiler_params=pltpu.CompilerParams(
            dimension_semantics=("parallel", "arbitrary"), vmem_limit_bytes=VMEM_LIMIT_BYTES),
        name="mlstm",
    )(h, g_pre.reshape(1, D_MODEL), g_post.reshape(1, D_MODEL), w_in, w_kt, w_grow,
      b_row, head_norm.reshape(1, ML_V), w_out)


def _lru_kernel(h_ref, gpre_ref, gpost_ref, win_ref, convw_ref, convb_ref, wgate_ref, ba_ref,
                bx_ref, lam_ref, wout_ref, o_ref, uext_ref, hc_ref, *, steps, batch):
    W = LRU_WIDTH
    rows = steps * batch
    halo = (CONV_WIDTH - 1) * batch

    @pl.when(pl.program_id(0) == 0)
    def _():
        uext_ref[0:halo, :] = jnp.zeros((halo, W), F32)
        hc_ref[...] = jnp.zeros_like(hc_ref)

    def time_major(v):
        return jnp.swapaxes(v.reshape(batch, steps, W), 0, 1).reshape(rows, W)

    def batch_major(v):
        return jnp.swapaxes(v.reshape(steps, batch, W), 0, 1).reshape(rows, W)

    h = h_ref[...].reshape(rows, D_MODEL)
    xn = _rms(h, gpre_ref[...]).astype(BF16)
    proj = _dot(xn, win_ref[...])
    gate_branch = proj[:, :W]
    u = time_major(proj[:, W:])

    uext_ref[halo:halo + rows, :] = u
    uc = convw_ref[CONV_WIDTH - 1:CONV_WIDTH, :] * u + convb_ref[...]
    for j in range(1, CONV_WIDTH):
        start = halo - j * batch
        uc = uc + convw_ref[CONV_WIDTH - 1 - j:CONV_WIDTH - j, :] * uext_ref[start:start + rows, :]
    uext_ref[0:halo, :] = u[rows - halo:rows, :]

    ucb = uc.astype(BF16)
    gates = [_dot(ucb[:, n * LRU_BLOCK:(n + 1) * LRU_BLOCK], wgate_ref[n]) for n in range(LRU_BLOCKS)]
    ga = jnp.concatenate([g[:, :LRU_BLOCK] for g in gates], axis=1)
    gx = jnp.concatenate([g[:, LRU_BLOCK:] for g in gates], axis=1)
    tr = jnp.tanh(ga + ba_ref[...])
    ti = jnp.tanh(gx + bx_ref[...])
    half_c = (-0.5 * LRU_C) * jax.nn.log_sigmoid(lam_ref[...])
    neg_log_a = tr * half_c + half_c
    a = jnp.exp2(neg_log_a * -LOG2E)
    gain2 = jnp.tanh(neg_log_a) * (a * a + 1.0)
    gain = gain2 * lax.rsqrt(jnp.maximum(gain2, jnp.finfo(F32).tiny))
    b2 = gain * uc * (ti + 1.0)

    carry = hc_ref[...]
    hs = []
    for t in range(steps):
        carry = a[t * batch:(t + 1) * batch, :] * carry + b2[t * batch:(t + 1) * batch, :]
        hs.append(carry)
    hc_ref[...] = carry
    hseq2 = batch_major(jnp.concatenate(hs, axis=0))

    x = gate_branch
    inner = x * (GELU_K1 * (x * x) + GELU_K0)
    y = (hseq2 * (0.25 * x) * (jnp.tanh(inner) + 1.0)).astype(BF16)
    out = h + _rms(_dot(y, wout_ref[...]), gpost_ref[...])
    o_ref[...] = out.reshape(batch, steps, D_MODEL)


def _lru_layer(h, g_pre, g_post, w_in, conv_w, conv_b, w_gate_a, b_gate_a, w_gate_x, b_gate_x,
               lam, w_out):
    B, S, _ = h.shape
    W = LRU_WIDTH
    steps = min(LRU_ROWS // B, S)
    halo = (CONV_WIDTH - 1) * B
    w_gate = (0.5 * jnp.concatenate([w_gate_a, w_gate_x], axis=-1)).astype(BF16)
    tok_spec = pl.BlockSpec((B, steps, D_MODEL), lambda i: (0, i, 0))
    row = lambda v: v.astype(F32).reshape(1, W)
    return pl.pallas_call(
        functools.partial(_lru_kernel, steps=steps, batch=B),
        grid=(S // steps,),
        in_specs=[tok_spec, _const_spec((1, D_MODEL)), _const_spec((1, D_MODEL)),
                  _const_spec((D_MODEL, 2 * W)), _const_spec((CONV_WIDTH, W)), _const_spec((1, W)),
                  _const_spec((LRU_BLOCKS, LRU_BLOCK, 2 * LRU_BLOCK)), _const_spec((1, W)),
                  _const_spec((1, W)), _const_spec((1, W)), _const_spec((W, D_MODEL))],
        out_specs=tok_spec,
        out_shape=jax.ShapeDtypeStruct((B, S, D_MODEL), F32),
        scratch_shapes=[pltpu.VMEM((steps * B + halo, W), F32), pltpu.VMEM((B, W), F32)],
        compiler_params=pltpu.CompilerParams(
            dimension_semantics=("arbitrary",), vmem_limit_bytes=VMEM_LIMIT_BYTES),
        name="rglru",
    )(h, g_pre.reshape(1, D_MODEL), g_post.reshape(1, D_MODEL),
      w_in.astype(BF16), conv_w.astype(F32), row(conv_b), w_gate, row(0.5 * b_gate_a),
      row(0.5 * b_gate_x), row(lam), w_out.astype(BF16))


def kernel(x, ml_w_in, ml_b_if, ml_head_norm, ml_w_out, lru_w_in, lru_conv_w, lru_conv_b, lru_w_gate_a, lru_b_gate_a, lru_w_gate_x, lru_b_gate_x, lru_lambda, lru_w_out, norm_pre_mix, norm_post_mix, norm_pre_ffn, norm_post_ffn, ffn_w_gate, ffn_w_up, ffn_w_down):
    B, S, D = x.shape
    depth = norm_pre_mix.shape[0]
    h = x
    for layer in range(depth):
        j = layer // 2
        if layer % 2 == 0:
            h = _mlstm_layer(h, norm_pre_mix[layer], norm_post_mix[layer], ml_w_in[j], ml_b_if[j],
                             ml_head_norm[j], ml_w_out[j])
        else:
            h = _lru_layer(h, norm_pre_mix[layer], norm_post_mix[layer], lru_w_in[j], lru_conv_w[j],
                           lru_conv_b[j], lru_w_gate_a[j], lru_b_gate_a[j], lru_w_gate_x[j],
                           lru_b_gate_x[j], lru_lambda[j], lru_w_out[j])
        h = _ffn_layer(h.reshape(B * S, D), norm_pre_ffn[layer], norm_post_ffn[layer],
                       ffn_w_gate[layer], ffn_w_up[layer], ffn_w_down[layer]).reshape(B, S, D)
    return h
```

```python
import functools
import math

import jax
import jax.numpy as jnp
from jax import lax
from jax.experimental import pallas as pl
from jax.experimental.pallas import tpu as pltpu

D_MODEL = 1024
ML_HEADS = 8
ML_QK_DIM = 64
ML_V_DIM = 128
ML_QK = ML_HEADS * ML_QK_DIM
ML_V = ML_HEADS * ML_V_DIM
GATE_CAP = 15.0
ML_M_INIT = -1e30
LRU_WIDTH = D_MODEL
LRU_BLOCKS = 4
LRU_BLOCK = LRU_WIDTH // LRU_BLOCKS
CONV_WIDTH = 4
LRU_C = 8.0
D_FF = 2816
EPS = 1e-6
LOG2E = math.log2(math.e)
GELU_K0 = math.sqrt(2.0 / math.pi)
GELU_K1 = GELU_K0 * 0.044715

SUBLANES = 8
LANES = 128
VMEM_LIMIT_BYTES = 56 * 1024 * 1024

ML_CHUNK = 128
ML_TILE = 512
LRU_ROWS = 512
FFN_TILE = 512
FFN_CHUNK = 256

F32 = jnp.float32
BF16 = jnp.bfloat16


def _rms(x, g):
    return x * lax.rsqrt(jnp.mean(x * x, axis=-1, keepdims=True) + EPS) * g


def _sigmoid(x):
    return 0.5 * jnp.tanh(0.5 * x) + 0.5


def _dot(a, b):
    return jnp.dot(a, b, preferred_element_type=F32)


def _dot_nt(a, b):
    return lax.dot_general(a, b, (((1,), (1,)), ((), ())), preferred_element_type=F32)


def _const_spec(shape):
    zeros = (0,) * len(shape)
    return pl.BlockSpec(shape, lambda *_: zeros, pipeline_mode=pl.Buffered(1))


def _layer_spec(stacked, layer):
    shape = stacked.shape[1:]
    zeros = (0,) * len(shape)
    return pl.BlockSpec((None,) + shape, lambda *_: (layer,) + zeros, pipeline_mode=pl.Buffered(1))


def _ffn_kernel(h_ref, gpre_ref, gpost_ref, wg_ref, wu_ref, wd_ref, o_ref, *, fc):
    h = h_ref[...]
    xn = _rms(h, gpre_ref[...])
    acc = None
    for c in range(D_FF // fc):
        cols = slice(c * fc, (c + 1) * fc)
        half_g = 0.5 * _dot(xn, wg_ref[:, cols])
        a = (jnp.tanh(half_g) * half_g + half_g) * _dot(xn, wu_ref[:, cols])
        part = _dot(a, wd_ref[cols, :])
        acc = part if acc is None else acc + part
    o_ref[...] = h + _rms(acc, gpost_ref[...])


def _ffn_layer(h, layer, g_pre, g_post, w_gate, w_up, w_down):
    n_tok = h.shape[0]
    tile = min(FFN_TILE, n_tok)
    tok_spec = pl.BlockSpec((tile, D_MODEL), lambda i: (i, 0))
    return pl.pallas_call(
        functools.partial(_ffn_kernel, fc=FFN_CHUNK),
        grid=(n_tok // tile,),
        in_specs=[tok_spec, _const_spec((1, D_MODEL)), _const_spec((1, D_MODEL)),
                  _layer_spec(w_gate, layer), _layer_spec(w_up, layer), _layer_spec(w_down, layer)],
        out_specs=tok_spec,
        out_shape=jax.ShapeDtypeStruct((n_tok, D_MODEL), F32),
        compiler_params=pltpu.CompilerParams(
            dimension_semantics=("parallel",), vmem_limit_bytes=VMEM_LIMIT_BYTES),
        name="swiglu",
    )(h, g_pre.reshape(1, D_MODEL), g_post.reshape(1, D_MODEL), w_gate, w_up, w_down)


def _soft_cap(z):
    return GATE_CAP * jnp.tanh(z / GATE_CAP)


def _lane_prefix(x, combine, identity):
    n = x.shape[-1]
    lane = lax.broadcasted_iota(jnp.int32, x.shape, x.ndim - 1)
    d = 1
    while d < n:
        x = combine(x, jnp.where(lane >= d, pltpu.roll(x, d, axis=x.ndim - 1), identity))
        d *= 2
    return x


def _mlstm_kernel(h_ref, gpre_ref, gpost_ref, win_ref, wkg_ref, brow_ref, hnorm_ref,
                  wout_ref, o_ref, c_ref, m_ref, hs_ref, *, tile, chunk):
    H, dk, dv, L = ML_HEADS, ML_QK_DIM, ML_V_DIM, chunk
    log2_q_scale = -0.5 * math.log2(dk)

    @pl.when(pl.program_id(1) == 0)
    def _():
        c_ref[...] = jnp.zeros_like(c_ref)
        m_ref[...] = jnp.full_like(m_ref, ML_M_INIT)

    h = h_ref[...]
    xn = _rms(h, gpre_ref[...])
    kg = _dot_nt(wkg_ref[...], xn)
    kt = kg[0:ML_QK, :]
    grow = _soft_cap(kg[ML_QK:ML_QK + 2 * H, :] + brow_ref[...])
    li_all = grow[0:H, :]
    lf_all = jax.nn.log_sigmoid(grow[H:2 * H, :])
    books = []
    m_prev = m_ref[...]
    for c in range(tile // L):
        li = li_all[:, c * L:(c + 1) * L]
        b = _lane_prefix(lf_all[:, c * L:(c + 1) * L], jnp.add, 0.0)
        rr = li - b
        rr_max = _lane_prefix(rr, jnp.maximum, -jnp.inf)
        g_tot = b[:, L - 1:L]
        m_t = b + jnp.maximum(rr_max, m_prev)
        m_new = jnp.maximum(g_tot + m_prev, g_tot + rr_max[:, L - 1:L])
        decay = jnp.exp(g_tot + m_prev - m_new)
        ea = jnp.exp(g_tot + rr - m_new)
        per_row = jnp.concatenate(
            [(b - m_t) * LOG2E + log2_q_scale, jnp.exp(b + m_prev - m_t) * 2.0 ** log2_q_scale,
             jnp.exp(-m_t), jnp.zeros((L - 3 * H, L), F32)], axis=0)
        books.append((per_row.T, rr * LOG2E, ea, decay))
        m_prev = m_new
    m_ref[...] = m_prev

    q_all = _dot(xn, win_ref[:, 0:ML_QK])
    vo = _dot(xn, win_ref[:, 2 * ML_QK:2 * ML_QK + 2 * ML_V])

    causal = (lax.broadcasted_iota(jnp.int32, (L, L), 1)
              <= lax.broadcasted_iota(jnp.int32, (L, L), 0))
    ones_aug = jnp.ones((L, dv), F32)
    zeros_kk = jnp.zeros((dk, dk), F32)

    for c in range(tile // L):
        r0 = c * L
        per_row, rr2, ea, decay = books[c]

        q = [q_all[r0:r0 + L, hh * dk:(hh + 1) * dk] for hh in range(H)]
        ktc = [kt[hh * dk:(hh + 1) * dk, r0:r0 + L] for hh in range(H)]
        s_mat = [_dot(q[hh], ktc[hh]) for hh in range(H)]

        res = []
        for hh in range(H):
            v_h = vo[r0:r0 + L, hh * dv:(hh + 1) * dv]
            cm2 = per_row[:, hh:hh + 1]
            w_inter = per_row[:, H + hh:H + hh + 1]
            c_prev = c_ref[hh]
            p_mat = jnp.where(causal, jnp.exp2(cm2 + rr2[hh:hh + 1, :]) * s_mat[hh], 0.0)
            lhs = jnp.concatenate(
                [jnp.concatenate([p_mat, q[hh] * w_inter], axis=1),
                 jnp.concatenate([ktc[hh] * ea[hh:hh + 1, :], zeros_kk], axis=1)], axis=0)
            rhs = jnp.concatenate([jnp.concatenate([v_h, ones_aug], axis=1), c_prev], axis=0)
            r = _dot(lhs, rhs)
            c_ref[hh] = decay[hh:hh + 1, :] * c_prev + r[L:L + dk]
            res.append(r)

        for hh in range(H):
            o_h = vo[r0:r0 + L, ML_V + hh * dv:ML_V + (hh + 1) * dv]
            emt = per_row[:, 2 * H + hh:2 * H + hh + 1]
            den = jnp.maximum(jnp.abs(res[hh][0:L, dv:2 * dv]), emt)
            h_out = res[hh][0:L, 0:dv] / den
            hn = h_out * lax.rsqrt(jnp.mean(h_out * h_out, axis=-1, keepdims=True) + EPS)
            hn = hn * hnorm_ref[:, hh * dv:(hh + 1) * dv]
            hs_ref[r0:r0 + L, hh * dv:(hh + 1) * dv] = hn * _sigmoid(o_h)

    y = _dot(hs_ref[...], wout_ref[...])
    o_ref[...] = h + _rms(y, gpost_ref[...])


def _mlstm_layer(h, layer, g_pre, g_post, w_in, b_if, head_norm, w_out):
    B, S, _ = h.shape
    H = ML_HEADS
    tile = min(ML_TILE, S)
    chunk = min(ML_CHUNK, tile)
    w_kg = jnp.concatenate([w_in[layer, :, ML_QK:2 * ML_QK], w_in[layer, :, 2 * ML_QK + 2 * ML_V:]], axis=1).T
    b_row = b_if.astype(F32).reshape(2 * H, 1)
    tok_spec = pl.BlockSpec((None, tile, D_MODEL), lambda b, t: (b, t, 0))
    return pl.pallas_call(
        functools.partial(_mlstm_kernel, tile=tile, chunk=chunk),
        grid=(B, S // tile),
        in_specs=[tok_spec, _const_spec((1, D_MODEL)), _const_spec((1, D_MODEL)),
                  _layer_spec(w_in, layer), _const_spec((ML_QK + 2 * H, D_MODEL)),
                  _const_spec((2 * H, 1)),
                  _const_spec((1, ML_V)), _layer_spec(w_out, layer)],
        out_specs=tok_spec,
        out_shape=jax.ShapeDtypeStruct((B, S, D_MODEL), F32),
        scratch_shapes=[pltpu.VMEM((H, ML_QK_DIM, 2 * ML_V_DIM), F32),
                        pltpu.VMEM((H, 1), F32),
                        pltpu.VMEM((tile, ML_V), F32)],
        compiler_params=pltpu.CompilerParams(
            dimension_semantics=("parallel", "arbitrary"), vmem_limit_bytes=VMEM_LIMIT_BYTES),
        name="mlstm",
    )(h, g_pre.reshape(1, D_MODEL), g_post.reshape(1, D_MODEL), w_in, w_kg,
      b_row, head_norm.reshape(1, ML_V), w_out)


def _lru_kernel(h_ref, gpre_ref, gpost_ref, win_ref, convw_ref, convb_ref, wgate_ref, ba_ref,
                bx_ref, lam_ref, wout_ref, o_ref, uext_ref, hc_ref, *, steps, batch):
    W = LRU_WIDTH
    rows = steps * batch
    halo = (CONV_WIDTH - 1) * batch

    @pl.when(pl.program_id(0) == 0)
    def _():
        uext_ref[0:halo, :] = jnp.zeros((halo, W), F32)
        hc_ref[...] = jnp.zeros_like(hc_ref)

    def time_major(v):
        return jnp.swapaxes(v.reshape(batch, steps, W), 0, 1).reshape(rows, W)

    def batch_major(v):
        return jnp.swapaxes(v.reshape(steps, batch, W), 0, 1).reshape(rows, W)

    h = h_ref[...].reshape(rows, D_MODEL)
    xn = _rms(h, gpre_ref[...]).astype(BF16)
    proj = _dot(xn, win_ref[...])
    gate_branch = proj[:, :W]
    u = time_major(proj[:, W:])

    uext_ref[halo:halo + rows, :] = u
    uc = convw_ref[CONV_WIDTH - 1:CONV_WIDTH, :] * u + convb_ref[...]
    for j in range(1, CONV_WIDTH):
        start = halo - j * batch
        uc = uc + convw_ref[CONV_WIDTH - 1 - j:CONV_WIDTH - j, :] * uext_ref[start:start + rows, :]
    uext_ref[0:halo, :] = u[rows - halo:rows, :]

    ucb = uc.astype(BF16)
    gates = [_dot(ucb[:, n * LRU_BLOCK:(n + 1) * LRU_BLOCK], wgate_ref[n]) for n in range(LRU_BLOCKS)]
    ga = jnp.concatenate([g[:, :LRU_BLOCK] for g in gates], axis=1)
    gx = jnp.concatenate([g[:, LRU_BLOCK:] for g in gates], axis=1)
    tr = jnp.tanh(ga + ba_ref[...])
    ti = jnp.tanh(gx + bx_ref[...])
    half_c = (-0.5 * LRU_C) * jax.nn.log_sigmoid(lam_ref[...])
    neg_log_a = tr * half_c + half_c
    a = jnp.exp2(neg_log_a * -LOG2E)
    gain2 = jnp.tanh(neg_log_a) * (a * a + 1.0)
    gain = gain2 * lax.rsqrt(jnp.maximum(gain2, jnp.finfo(F32).tiny))
    b2 = gain * uc * (ti + 1.0)

    carry = hc_ref[...]
    hs = []
    for t in range(steps):
        carry = a[t * batch:(t + 1) * batch, :] * carry + b2[t * batch:(t + 1) * batch, :]
        hs.append(carry)
    hc_ref[...] = carry
    hseq2 = batch_major(jnp.concatenate(hs, axis=0))

    x = gate_branch
    inner = x * (GELU_K1 * (x * x) + GELU_K0)
    y = (hseq2 * (0.25 * x) * (jnp.tanh(inner) + 1.0)).astype(BF16)
    out = h + _rms(_dot(y, wout_ref[...]), gpost_ref[...])
    o_ref[...] = out.reshape(batch, steps, D_MODEL)


def _lru_layer(h, g_pre, g_post, w_in, conv_w, conv_b, w_gate_a, b_gate_a, w_gate_x, b_gate_x,
               lam, w_out):
    B, S, _ = h.shape
    W = LRU_WIDTH
    steps = min(LRU_ROWS // B, S)
    halo = (CONV_WIDTH - 1) * B
    w_gate = (0.5 * jnp.concatenate([w_gate_a, w_gate_x], axis=-1)).astype(BF16)
    tok_spec = pl.BlockSpec((B, steps, D_MODEL), lambda i: (0, i, 0))
    row = lambda v: v.astype(F32).reshape(1, W)
    return pl.pallas_call(
        functools.partial(_lru_kernel, steps=steps, batch=B),
        grid=(S // steps,),
        in_specs=[tok_spec, _const_spec((1, D_MODEL)), _const_spec((1, D_MODEL)),
                  _const_spec((D_MODEL, 2 * W)), _const_spec((CONV_WIDTH, W)), _const_spec((1, W)),
                  _const_spec((LRU_BLOCKS, LRU_BLOCK, 2 * LRU_BLOCK)), _const_spec((1, W)),
                  _const_spec((1, W)), _const_spec((1, W)), _const_spec((W, D_MODEL))],
        out_specs=tok_spec,
        out_shape=jax.ShapeDtypeStruct((B, S, D_MODEL), F32),
        scratch_shapes=[pltpu.VMEM((steps * B + halo, W), F32), pltpu.VMEM((B, W), F32)],
        compiler_params=pltpu.CompilerParams(
            dimension_semantics=("arbitrary",), vmem_limit_bytes=VMEM_LIMIT_BYTES),
        name="rglru",
    )(h, g_pre.reshape(1, D_MODEL), g_post.reshape(1, D_MODEL),
      w_in.astype(BF16), conv_w.astype(F32), row(conv_b), w_gate, row(0.5 * b_gate_a),
      row(0.5 * b_gate_x), row(lam), w_out.astype(BF16))


def kernel(x, ml_w_in, ml_b_if, ml_head_norm, ml_w_out, lru_w_in, lru_conv_w, lru_conv_b, lru_w_gate_a, lru_b_gate_a, lru_w_gate_x, lru_b_gate_x, lru_lambda, lru_w_out, norm_pre_mix, norm_post_mix, norm_pre_ffn, norm_post_ffn, ffn_w_gate, ffn_w_up, ffn_w_down):
    B, S, D = x.shape
    depth = norm_pre_mix.shape[0]
    h = x
    for layer in range(depth):
        j = layer // 2
        if layer % 2 == 0:
            h = _mlstm_layer(h, j, norm_pre_mix[layer], norm_post_mix[layer], ml_w_in, ml_b_if[j],
                             ml_head_norm[j], ml_w_out)
        else:
            h = _lru_layer(h, norm_pre_mix[layer], norm_post_mix[layer], lru_w_in[j], lru_conv_w[j],
                           lru_conv_b[j], lru_w_gate_a[j], lru_b_gate_a[j], lru_w_gate_x[j],
                           lru_b_gate_x[j], lru_lambda[j], lru_w_out[j])
        h = _ffn_layer(h.reshape(B * S, D), layer, norm_pre_ffn[layer], norm_post_ffn[layer],
                       ffn_w_gate, ffn_w_up, ffn_w_down).reshape(B, S, D)
    return h
```

```python
import functools
import math

import jax
import jax.numpy as jnp
from jax import lax
from jax.experimental import pallas as pl
from jax.experimental.pallas import tpu as pltpu

D_MODEL = 1024
ML_HEADS = 8
ML_QK_DIM = 64
ML_V_DIM = 128
ML_QK = ML_HEADS * ML_QK_DIM
ML_V = ML_HEADS * ML_V_DIM
GATE_CAP = 15.0
ML_M_INIT = -1e30
LRU_WIDTH = D_MODEL
LRU_BLOCKS = 4
LRU_BLOCK = LRU_WIDTH // LRU_BLOCKS
CONV_WIDTH = 4
LRU_C = 8.0
D_FF = 2816
EPS = 1e-6
LOG2E = math.log2(math.e)
GELU_K0 = math.sqrt(2.0 / math.pi)
GELU_K1 = GELU_K0 * 0.044715

SUBLANES = 8
LANES = 128
VMEM_LIMIT_BYTES = 56 * 1024 * 1024

ML_CHUNK = 128
ML_TILE = 512
LRU_ROWS = 512
FFN_TILE = 512
FFN_CHUNK = 256

F32 = jnp.float32
BF16 = jnp.bfloat16


def _rms(x, g):
    return x * lax.rsqrt(jnp.mean(x * x, axis=-1, keepdims=True) + EPS) * g


def _sigmoid(x):
    return 0.5 * jnp.tanh(0.5 * x) + 0.5


def _dot(a, b):
    return jnp.dot(a, b, preferred_element_type=F32)


def _dot_nt(a, b):
    return lax.dot_general(a, b, (((1,), (1,)), ((), ())), preferred_element_type=F32)


def _const_spec(shape):
    zeros = (0,) * len(shape)
    return pl.BlockSpec(shape, lambda *_: zeros, pipeline_mode=pl.Buffered(1))


def _layer_spec(stacked, layer):
    shape = stacked.shape[1:]
    zeros = (0,) * len(shape)
    return pl.BlockSpec((None,) + shape, lambda *_: (layer,) + zeros, pipeline_mode=pl.Buffered(1))


def _ffn_kernel(h_ref, gpre_ref, gpost_ref, wg_ref, wu_ref, wd_ref, o_ref, *, fc):
    h = h_ref[...]
    xn = _rms(h, gpre_ref[...])
    acc = None
    for c in range(D_FF // fc):
        cols = slice(c * fc, (c + 1) * fc)
        half_g = 0.5 * _dot(xn, wg_ref[:, cols])
        a = (jnp.tanh(half_g) * half_g + half_g) * _dot(xn, wu_ref[:, cols])
        part = _dot(a, wd_ref[cols, :])
        acc = part if acc is None else acc + part
    o_ref[...] = h + _rms(acc, gpost_ref[...])


def _ffn_layer(h, layer, g_pre, g_post, w_gate, w_up, w_down):
    n_tok = h.shape[0]
    tile = min(FFN_TILE, n_tok)
    tok_spec = pl.BlockSpec((tile, D_MODEL), lambda i: (i, 0))
    return pl.pallas_call(
        functools.partial(_ffn_kernel, fc=FFN_CHUNK),
        grid=(n_tok // tile,),
        in_specs=[tok_spec, _const_spec((1, D_MODEL)), _const_spec((1, D_MODEL)),
                  _layer_spec(w_gate, layer), _layer_spec(w_up, layer), _layer_spec(w_down, layer)],
        out_specs=tok_spec,
        out_shape=jax.ShapeDtypeStruct((n_tok, D_MODEL), F32),
        compiler_params=pltpu.CompilerParams(
            dimension_semantics=("parallel",), vmem_limit_bytes=VMEM_LIMIT_BYTES),
        name="swiglu",
    )(h, g_pre.reshape(1, D_MODEL), g_post.reshape(1, D_MODEL), w_gate, w_up, w_down)


def _soft_cap(z):
    return GATE_CAP * jnp.tanh(z / GATE_CAP)


def _lane_prefix(x, combine, identity):
    n = x.shape[-1]
    lane = lax.broadcasted_iota(jnp.int32, x.shape, x.ndim - 1)
    d = 1
    while d < n:
        x = combine(x, jnp.where(lane >= d, pltpu.roll(x, d, axis=x.ndim - 1), identity))
        d *= 2
    return x


def _mlstm_kernel(h_ref, gpre_ref, gpost_ref, win_ref, wkg_ref, brow_ref, hnorm_ref,
                  wout_ref, o_ref, c_ref, m_ref, hs_ref, *, tile, chunk):
    H, dk, dv, L = ML_HEADS, ML_QK_DIM, ML_V_DIM, chunk
    log2_q_scale = -0.5 * math.log2(dk)

    @pl.when(pl.program_id(1) == 0)
    def _():
        c_ref[...] = jnp.zeros_like(c_ref)
        m_ref[...] = jnp.full_like(m_ref, ML_M_INIT)

    h = h_ref[...]
    xn = _rms(h, gpre_ref[...])
    kg = _dot_nt(wkg_ref[...], xn)
    kt = kg[0:ML_QK, :]
    grow = _soft_cap(kg[ML_QK:ML_QK + 2 * H, :] + brow_ref[...])
    li_all = grow[0:H, :]
    lf_all = jax.nn.log_sigmoid(grow[H:2 * H, :])
    books = []
    m_prev = m_ref[...]
    for c in range(tile // L):
        li = li_all[:, c * L:(c + 1) * L]
        b = _lane_prefix(lf_all[:, c * L:(c + 1) * L], jnp.add, 0.0)
        rr = li - b
        rr_max = _lane_prefix(rr, jnp.maximum, -jnp.inf)
        g_tot = b[:, L - 1:L]
        m_t = b + jnp.maximum(rr_max, m_prev)
        m_new = jnp.maximum(g_tot + m_prev, g_tot + rr_max[:, L - 1:L])
        decay = jnp.exp(g_tot + m_prev - m_new)
        ea = jnp.exp(g_tot + rr - m_new)
        per_row = jnp.concatenate(
            [(b - m_t) * LOG2E + log2_q_scale, jnp.exp(b + m_prev - m_t) * 2.0 ** log2_q_scale,
             jnp.exp(-m_t), jnp.zeros((L - 3 * H, L), F32)], axis=0)
        books.append((per_row.T, rr * LOG2E, ea, decay))
        m_prev = m_new
    m_ref[...] = m_prev

    gate = _sigmoid(_dot(xn, win_ref[:, 2 * ML_QK + ML_V:2 * ML_QK + 2 * ML_V])) * hnorm_ref[...]
    q_all = _dot(xn, win_ref[:, 0:ML_QK])
    v_all = _dot(xn, win_ref[:, 2 * ML_QK:2 * ML_QK + ML_V])

    causal = (lax.broadcasted_iota(jnp.int32, (L, L), 1)
              <= lax.broadcasted_iota(jnp.int32, (L, L), 0))
    ones_aug = jnp.ones((L, dv), F32)
    zeros_kk = jnp.zeros((dk, dk), F32)

    for c in range(tile // L):
        r0 = c * L
        per_row, rr2, ea, decay = books[c]

        q = [q_all[r0:r0 + L, hh * dk:(hh + 1) * dk] for hh in range(H)]
        ktc = [kt[hh * dk:(hh + 1) * dk, r0:r0 + L] for hh in range(H)]
        s_mat = [_dot(q[hh], ktc[hh]) for hh in range(H)]

        res = []
        for hh in range(H):
            v_h = v_all[r0:r0 + L, hh * dv:(hh + 1) * dv]
            cm2 = per_row[:, hh:hh + 1]
            w_inter = per_row[:, H + hh:H + hh + 1]
            c_prev = c_ref[hh]
            p_mat = jnp.where(causal, jnp.exp2(cm2 + rr2[hh:hh + 1, :]) * s_mat[hh], 0.0)
            lhs = jnp.concatenate(
                [jnp.concatenate([p_mat, q[hh] * w_inter], axis=1),
                 jnp.concatenate([ktc[hh] * ea[hh:hh + 1, :], zeros_kk], axis=1)], axis=0)
            rhs = jnp.concatenate([jnp.concatenate([v_h, ones_aug], axis=1), c_prev], axis=0)
            r = _dot(lhs, rhs)
            c_ref[hh] = decay[hh:hh + 1, :] * c_prev + r[L:L + dk]
            res.append(r)

        for hh in range(H):
            emt = per_row[:, 2 * H + hh:2 * H + hh + 1]
            den = jnp.maximum(jnp.abs(res[hh][0:L, dv:2 * dv]), emt)
            h_out = res[hh][0:L, 0:dv] / den
            hn = h_out * lax.rsqrt(jnp.mean(h_out * h_out, axis=-1, keepdims=True) + EPS)
            hs_ref[r0:r0 + L, hh * dv:(hh + 1) * dv] = hn * gate[r0:r0 + L, hh * dv:(hh + 1) * dv]

    y = _dot(hs_ref[...], wout_ref[...])
    o_ref[...] = h + _rms(y, gpost_ref[...])


def _mlstm_layer(h, layer, g_pre, g_post, w_in, b_if, head_norm, w_out):
    B, S, _ = h.shape
    H = ML_HEADS
    tile = min(ML_TILE, S)
    chunk = min(ML_CHUNK, tile)
    w_kg = jnp.concatenate([w_in[layer, :, ML_QK:2 * ML_QK], w_in[layer, :, 2 * ML_QK + 2 * ML_V:]], axis=1).T
    b_row = b_if.astype(F32).reshape(2 * H, 1)
    tok_spec = pl.BlockSpec((None, tile, D_MODEL), lambda b, t: (b, t, 0))
    return pl.pallas_call(
        functools.partial(_mlstm_kernel, tile=tile, chunk=chunk),
        grid=(B, S // tile),
        in_specs=[tok_spec, _const_spec((1, D_MODEL)), _const_spec((1, D_MODEL)),
                  _layer_spec(w_in, layer), _const_spec((ML_QK + 2 * H, D_MODEL)),
                  _const_spec((2 * H, 1)),
                  _const_spec((1, ML_V)), _layer_spec(w_out, layer)],
        out_specs=tok_spec,
        out_shape=jax.ShapeDtypeStruct((B, S, D_MODEL), F32),
        scratch_shapes=[pltpu.VMEM((H, ML_QK_DIM, 2 * ML_V_DIM), F32),
                        pltpu.VMEM((H, 1), F32),
                        pltpu.VMEM((tile, ML_V), F32)],
        compiler_params=pltpu.CompilerParams(
            dimension_semantics=("parallel", "arbitrary"), vmem_limit_bytes=VMEM_LIMIT_BYTES),
        name="mlstm",
    )(h, g_pre.reshape(1, D_MODEL), g_post.reshape(1, D_MODEL), w_in, w_kg,
      b_row, head_norm.reshape(1, ML_V), w_out)


def _lru_kernel(h_ref, gpre_ref, gpost_ref, win_ref, convw_ref, convb_ref, wgate_ref, ba_ref,
                bx_ref, lam_ref, wout_ref, o_ref, uext_ref, hc_ref, *, steps, batch):
    W = LRU_WIDTH
    rows = steps * batch
    halo = (CONV_WIDTH - 1) * batch

    @pl.when(pl.program_id(0) == 0)
    def _():
        uext_ref[0:halo, :] = jnp.zeros((halo, W), F32)
        hc_ref[...] = jnp.zeros_like(hc_ref)

    def time_major(v):
        return jnp.swapaxes(v.reshape(batch, steps, W), 0, 1).reshape(rows, W)

    def batch_major(v):
        return jnp.swapaxes(v.reshape(steps, batch, W), 0, 1).reshape(rows, W)

    h = h_ref[...].reshape(rows, D_MODEL)
    xn = _rms(h, gpre_ref[...]).astype(BF16)
    proj = _dot(xn, win_ref[...])
    gate_branch = proj[:, :W]
    u = time_major(proj[:, W:])

    uext_ref[halo:halo + rows, :] = u
    uc = convw_ref[CONV_WIDTH - 1:CONV_WIDTH, :] * u + convb_ref[...]
    for j in range(1, CONV_WIDTH):
        start = halo - j * batch
        uc = uc + convw_ref[CONV_WIDTH - 1 - j:CONV_WIDTH - j, :] * uext_ref[start:start + rows, :]
    uext_ref[0:halo, :] = u[rows - halo:rows, :]

    ucb = uc.astype(BF16)
    gates = [_dot(ucb[:, n * LRU_BLOCK:(n + 1) * LRU_BLOCK], wgate_ref[n]) for n in range(LRU_BLOCKS)]
    ga = jnp.concatenate([g[:, :LRU_BLOCK] for g in gates], axis=1)
    gx = jnp.concatenate([g[:, LRU_BLOCK:] for g in gates], axis=1)
    tr = jnp.tanh(ga + ba_ref[...])
    ti = jnp.tanh(gx + bx_ref[...])
    half_c = (-0.5 * LRU_C) * jax.nn.log_sigmoid(lam_ref[...])
    neg_log_a = tr * half_c + half_c
    a = jnp.exp2(neg_log_a * -LOG2E)
    gain2 = jnp.tanh(neg_log_a) * (a * a + 1.0)
    gain = gain2 * lax.rsqrt(jnp.maximum(gain2, jnp.finfo(F32).tiny))
    b2 = gain * uc * (ti + 1.0)

    carry = hc_ref[...]
    hs = []
    for t in range(steps):
        carry = a[t * batch:(t + 1) * batch, :] * carry + b2[t * batch:(t + 1) * batch, :]
        hs.append(carry)
    hc_ref[...] = carry
    hseq2 = batch_major(jnp.concatenate(hs, axis=0))

    x = gate_branch
    inner = x * (GELU_K1 * (x * x) + GELU_K0)
    y = (hseq2 * (0.25 * x) * (jnp.tanh(inner) + 1.0)).astype(BF16)
    out = h + _rms(_dot(y, wout_ref[...]), gpost_ref[...])
    o_ref[...] = out.reshape(batch, steps, D_MODEL)


def _lru_layer(h, g_pre, g_post, w_in, conv_w, conv_b, w_gate_a, b_gate_a, w_gate_x, b_gate_x,
               lam, w_out):
    B, S, _ = h.shape
    W = LRU_WIDTH
    steps = min(LRU_ROWS // B, S)
    halo = (CONV_WIDTH - 1) * B
    w_gate = (0.5 * jnp.concatenate([w_gate_a, w_gate_x], axis=-1)).astype(BF16)
    tok_spec = pl.BlockSpec((B, steps, D_MODEL), lambda i: (0, i, 0))
    row = lambda v: v.astype(F32).reshape(1, W)
    return pl.pallas_call(
        functools.partial(_lru_kernel, steps=steps, batch=B),
        grid=(S // steps,),
        in_specs=[tok_spec, _const_spec((1, D_MODEL)), _const_spec((1, D_MODEL)),
                  _const_spec((D_MODEL, 2 * W)), _const_spec((CONV_WIDTH, W)), _const_spec((1, W)),
                  _const_spec((LRU_BLOCKS, LRU_BLOCK, 2 * LRU_BLOCK)), _const_spec((1, W)),
                  _const_spec((1, W)), _const_spec((1, W)), _const_spec((W, D_MODEL))],
        out_specs=tok_spec,
        out_shape=jax.ShapeDtypeStruct((B, S, D_MODEL), F32),
        scratch_shapes=[pltpu.VMEM((steps * B + halo, W), F32), pltpu.VMEM((B, W), F32)],
        compiler_params=pltpu.CompilerParams(
            dimension_semantics=("arbitrary",), vmem_limit_bytes=VMEM_LIMIT_BYTES),
        name="rglru",
    )(h, g_pre.reshape(1, D_MODEL), g_post.reshape(1, D_MODEL),
      w_in.astype(BF16), conv_w.astype(F32), row(conv_b), w_gate, row(0.5 * b_gate_a),
      row(0.5 * b_gate_x), row(lam), w_out.astype(BF16))


def kernel(x, ml_w_in, ml_b_if, ml_head_norm, ml_w_out, lru_w_in, lru_conv_w, lru_conv_b, lru_w_gate_a, lru_b_gate_a, lru_w_gate_x, lru_b_gate_x, lru_lambda, lru_w_out, norm_pre_mix, norm_post_mix, norm_pre_ffn, norm_post_ffn, ffn_w_gate, ffn_w_up, ffn_w_down):
    B, S, D = x.shape
    depth = norm_pre_mix.shape[0]
    h = x
    for layer in range(depth):
        j = layer // 2
        if layer % 2 == 0:
            h = _mlstm_layer(h, j, norm_pre_mix[layer], norm_post_mix[layer], ml_w_in, ml_b_if[j],
                             ml_head_norm[j], ml_w_out)
        else:
            h = _lru_layer(h, norm_pre_mix[layer], norm_post_mix[layer], lru_w_in[j], lru_conv_w[j],
                           lru_conv_b[j], lru_w_gate_a[j], lru_b_gate_a[j], lru_w_gate_x[j],
                           lru_b_gate_x[j], lru_lambda[j], lru_w_out[j])
        h = _ffn_layer(h.reshape(B * S, D), layer, norm_pre_ffn[layer], norm_post_ffn[layer],
                       ffn_w_gate, ffn_w_up, ffn_w_down).reshape(B, S, D)
    return h
```

```python
import functools
import math

import jax
import jax.numpy as jnp
from jax import lax
from jax.experimental import pallas as pl
from jax.experimental.pallas import tpu as pltpu

D_MODEL = 1024
ML_HEADS = 8
ML_QK_DIM = 64
ML_V_DIM = 128
ML_QK = ML_HEADS * ML_QK_DIM
ML_V = ML_HEADS * ML_V_DIM
GATE_CAP = 15.0
ML_M_INIT = -1e30
LRU_WIDTH = D_MODEL
LRU_BLOCKS = 4
LRU_BLOCK = LRU_WIDTH // LRU_BLOCKS
CONV_WIDTH = 4
LRU_C = 8.0
D_FF = 2816
EPS = 1e-6
LOG2E = math.log2(math.e)
GELU_K0 = math.sqrt(2.0 / math.pi)
GELU_K1 = GELU_K0 * 0.044715

SUBLANES = 8
LANES = 128
VMEM_LIMIT_BYTES = 56 * 1024 * 1024

ML_CHUNK = 128
ML_TILE = 512
LRU_ROWS = 512
FFN_TILE = 512
FFN_CHUNK = 256

F32 = jnp.float32
BF16 = jnp.bfloat16


def _rms(x, g):
    return x * lax.rsqrt(jnp.mean(x * x, axis=-1, keepdims=True) + EPS) * g


def _sigmoid(x):
    return 0.5 * jnp.tanh(0.5 * x) + 0.5


def _dot(a, b):
    return jnp.dot(a, b, preferred_element_type=F32)


def _dot_nt(a, b):
    return lax.dot_general(a, b, (((1,), (1,)), ((), ())), preferred_element_type=F32)


def _const_spec(shape):
    zeros = (0,) * len(shape)
    return pl.BlockSpec(shape, lambda *_: zeros, pipeline_mode=pl.Buffered(1))


def _layer_spec(stacked, layer):
    shape = stacked.shape[1:]
    zeros = (0,) * len(shape)
    return pl.BlockSpec((None,) + shape, lambda *_: (layer,) + zeros, pipeline_mode=pl.Buffered(1))


def _ffn_kernel(h_ref, gpre_ref, gpost_ref, wg_ref, wu_ref, wd_ref, o_ref, *, fc):
    h = h_ref[...]
    xn = _rms(h, gpre_ref[...])
    acc = None
    for c in range(D_FF // fc):
        cols = slice(c * fc, (c + 1) * fc)
        half_g = 0.5 * _dot(xn, wg_ref[:, cols])
        a = (jnp.tanh(half_g) * half_g + half_g) * _dot(xn, wu_ref[:, cols])
        part = _dot(a, wd_ref[cols, :])
        acc = part if acc is None else acc + part
    o_ref[...] = h + _rms(acc, gpost_ref[...])


def _ffn_layer(h, layer, g_pre, g_post, w_gate, w_up, w_down):
    n_tok = h.shape[0]
    tile = min(FFN_TILE, n_tok)
    tok_spec = pl.BlockSpec((tile, D_MODEL), lambda i: (i, 0))
    return pl.pallas_call(
        functools.partial(_ffn_kernel, fc=FFN_CHUNK),
        grid=(n_tok // tile,),
        in_specs=[tok_spec, _const_spec((1, D_MODEL)), _const_spec((1, D_MODEL)),
                  _layer_spec(w_gate, layer), _layer_spec(w_up, layer), _layer_spec(w_down, layer)],
        out_specs=tok_spec,
        out_shape=jax.ShapeDtypeStruct((n_tok, D_MODEL), F32),
        compiler_params=pltpu.CompilerParams(
            dimension_semantics=("parallel",), vmem_limit_bytes=VMEM_LIMIT_BYTES),
        name="swiglu",
    )(h, g_pre.reshape(1, D_MODEL), g_post.reshape(1, D_MODEL), w_gate, w_up, w_down)


def _soft_cap(z):
    return GATE_CAP * jnp.tanh(z / GATE_CAP)


def _lane_prefix(x, combine, identity):
    n = x.shape[-1]
    lane = lax.broadcasted_iota(jnp.int32, x.shape, x.ndim - 1)
    d = 1
    while d < n:
        x = combine(x, jnp.where(lane >= d, pltpu.roll(x, d, axis=x.ndim - 1), identity))
        d *= 2
    return x


def _mlstm_step(hx_ref, hy_ref, gpre_ref, gpost_ref, win_ref, wkg_ref, brow_ref, hnorm_ref, wout_ref,
                write, read, o_ref, c_ref, m_ref, hs_ref, *, tile, chunk):
    H, dk, dv, L = ML_HEADS, ML_QK_DIM, ML_V_DIM, chunk
    n_chunks = tile // L
    log2_q_scale = -0.5 * math.log2(dk)
    q_w, v_w, gate_w, kt_w, prt_w, rr2_w, ea_w, dec_w = write
    q_r, v_r, gate_r, kt_r, prt_r, rr2_r, ea_r, dec_r = read

    causal = (lax.broadcasted_iota(jnp.int32, (L, L), 1)
              <= lax.broadcasted_iota(jnp.int32, (L, L), 0))
    ones_aug = jnp.ones((L, dv), F32)
    zeros_kk = jnp.zeros((dk, dk), F32)

    def scores(c):
        r0 = c * L
        return [_dot(q_r[hh, r0:r0 + L, :], kt_r[hh * dk:(hh + 1) * dk, r0:r0 + L]) for hh in range(H)]

    def state_matmul(c, hh, s_mat):
        r0 = c * L
        q_h = q_r[hh, r0:r0 + L, :]
        kt_h = kt_r[hh * dk:(hh + 1) * dk, r0:r0 + L]
        v_h = v_r[r0:r0 + L, hh * dv:(hh + 1) * dv]
        cm2 = prt_r[c, :, hh:hh + 1]
        w_inter = prt_r[c, :, H + hh:H + hh + 1]
        c_prev = c_ref[hh]
        p_mat = jnp.where(causal, jnp.exp2(cm2 + rr2_r[hh:hh + 1, r0:r0 + L]) * s_mat, 0.0)
        lhs = jnp.concatenate(
            [jnp.concatenate([p_mat, q_h * w_inter], axis=1),
             jnp.concatenate([kt_h * ea_r[hh:hh + 1, r0:r0 + L], zeros_kk], axis=1)], axis=0)
        rhs = jnp.concatenate([jnp.concatenate([v_h, ones_aug], axis=1), c_prev], axis=0)
        r = _dot(lhs, rhs)
        c_ref[hh] = dec_r[c, hh:hh + 1, :] * c_prev + r[L:L + dk]
        return r

    def epilogue(c, hh, r):
        r0 = c * L
        emt = prt_r[c, :, 2 * H + hh:2 * H + hh + 1]
        den = jnp.maximum(jnp.abs(r[0:L, dv:2 * dv]), emt)
        h_out = r[0:L, 0:dv] / den
        hn = h_out * lax.rsqrt(jnp.mean(h_out * h_out, axis=-1, keepdims=True) + EPS)
        hs_ref[r0:r0 + L, hh * dv:(hh + 1) * dv] = hn * gate_r[r0:r0 + L, hh * dv:(hh + 1) * dv]

    PW = 2 * LANES
    gate_rows = []

    def kg_piece(t0):
        def run():
            kg = _dot_nt(wkg_ref[...], xn[t0:t0 + PW, :])
            kt_w[:, t0:t0 + PW] = kg[0:ML_QK, :]
            gate_rows.append(_soft_cap(kg[ML_QK:ML_QK + 2 * H, :] + brow_ref[...]))
        return run

    def q_piece(c0):
        def run():
            q4 = _dot(xn, win_ref[:, c0:c0 + PW])
            for j in range(PW // dk):
                q_w[c0 // dk + j] = q4[:, j * dk:(j + 1) * dk]
        return run

    def v_piece(c0):
        def run():
            v_w[:, c0:c0 + PW] = _dot(xn, win_ref[:, 2 * ML_QK + c0:2 * ML_QK + c0 + PW])
        return run

    def gate_piece(c0):
        def run():
            o = _dot(xn, win_ref[:, 2 * ML_QK + ML_V + c0:2 * ML_QK + ML_V + c0 + PW])
            gate_w[:, c0:c0 + PW] = _sigmoid(o) * hnorm_ref[:, c0:c0 + PW]
        return run

    def bookkeeping():
        grow = jnp.concatenate(gate_rows, axis=1)
        li_all = grow[0:H, :]
        lf_all = jax.nn.log_sigmoid(grow[H:2 * H, :])
        m_prev = m_ref[...]
        for c in range(n_chunks):
            li = li_all[:, c * L:(c + 1) * L]
            b = _lane_prefix(lf_all[:, c * L:(c + 1) * L], jnp.add, 0.0)
            rr = li - b
            rr_max = _lane_prefix(rr, jnp.maximum, -jnp.inf)
            g_tot = b[:, L - 1:L]
            m_t = b + jnp.maximum(rr_max, m_prev)
            m_new = jnp.maximum(g_tot + m_prev, g_tot + rr_max[:, L - 1:L])
            dec_w[c] = jnp.exp(g_tot + m_prev - m_new)
            ea_w[:, c * L:(c + 1) * L] = jnp.exp(g_tot + rr - m_new)
            rr2_w[:, c * L:(c + 1) * L] = rr * LOG2E
            per_row = jnp.concatenate(
                [(b - m_t) * LOG2E + log2_q_scale, jnp.exp(b + m_prev - m_t) * 2.0 ** log2_q_scale,
                 jnp.exp(-m_t), jnp.zeros((L - 3 * H, L), F32)], axis=0)
            prt_w[c] = per_row.T
            m_prev = m_new
        m_ref[...] = m_prev

    pieces = ([kg_piece(t0) for t0 in range(0, tile, PW)] + [q_piece(c0) for c0 in range(0, ML_QK, PW)]
              + [v_piece(c0) for c0 in range(0, ML_V, PW)] + [gate_piece(c0) for c0 in range(0, ML_V, PW)])
    slots = n_chunks * 3
    per_slot = -(-len(pieces) // slots)

    def issue_pieces():
        for _ in range(per_slot):
            if pieces:
                pieces.pop(0)()

    s_mat = scores(0)
    xn = _rms(hx_ref[...], gpre_ref[...])
    for c in range(n_chunks):
        res = {}
        for hh in range(H):
            res[hh] = state_matmul(c, hh, s_mat[hh])
            if hh in (1, 4, 7):
                issue_pieces()
                for done in (hh - 2, hh - 1, hh) if hh > 1 else (0, 1):
                    epilogue(c, done, res[done])
        if c + 1 < n_chunks:
            s_mat = scores(c + 1)
    while pieces:
        pieces.pop(0)()
    bookkeeping()

    y = _dot(hs_ref[...], wout_ref[...])
    o_ref[...] = hy_ref[...] + _rms(y, gpost_ref[...])


def _mlstm_kernel(hx_ref, hy_ref, gpre_ref, gpost_ref, win_ref, wkg_ref, brow_ref, hnorm_ref,
                  wout_ref, o_ref, c_ref, m_ref, hs_ref, *proj_refs, tile, chunk, tiles_per_seq):
    s = pl.program_id(0)
    n_buf = len(proj_refs) // 2
    bufs = (proj_refs[:n_buf], proj_refs[n_buf:])

    @pl.when(s == 0)
    def _():
        for ref in bufs[1]:
            ref[...] = jnp.zeros_like(ref)

    @pl.when(s % tiles_per_seq == 0)
    def _():
        m_ref[...] = jnp.full_like(m_ref, ML_M_INIT)

    @pl.when(jnp.maximum(s - 1, 0) % tiles_per_seq == 0)
    def _():
        c_ref[...] = jnp.zeros_like(c_ref)

    step = functools.partial(
        _mlstm_step, hx_ref, hy_ref, gpre_ref, gpost_ref, win_ref, wkg_ref, brow_ref, hnorm_ref,
        wout_ref, o_ref=o_ref, c_ref=c_ref, m_ref=m_ref, hs_ref=hs_ref, tile=tile, chunk=chunk)
    pl.when(s % 2 == 0)(lambda: step(bufs[0], bufs[1]))
    pl.when(s % 2 == 1)(lambda: step(bufs[1], bufs[0]))


def _mlstm_layer(h, layer, g_pre, g_post, w_in, b_if, head_norm, w_out):
    B, S, _ = h.shape
    H = ML_HEADS
    tile = min(ML_TILE, S)
    chunk = min(ML_CHUNK, tile)
    tiles_per_seq = S // tile
    n_tiles = B * tiles_per_seq
    w_kg = jnp.concatenate([w_in[layer, :, ML_QK:2 * ML_QK], w_in[layer, :, 2 * ML_QK + 2 * ML_V:]], axis=1).T
    b_row = b_if.astype(F32).reshape(2 * H, 1)

    def ahead(s):
        k = jnp.minimum(s, n_tiles - 1)
        return (k // tiles_per_seq, k % tiles_per_seq, 0)

    def behind(s):
        k = jnp.maximum(s - 1, 0)
        return (k // tiles_per_seq, k % tiles_per_seq, 0)

    block = (None, tile, D_MODEL)
    n_chunks = tile // chunk
    stage_bufs = [pltpu.VMEM((H, tile, ML_QK_DIM), F32),
                  pltpu.VMEM((tile, ML_V), F32),
                  pltpu.VMEM((tile, ML_V), F32),
                  pltpu.VMEM((ML_QK, tile), F32),
                  pltpu.VMEM((n_chunks, chunk, LANES), F32),
                  pltpu.VMEM((H, tile), F32),
                  pltpu.VMEM((H, tile), F32),
                  pltpu.VMEM((n_chunks, H, 1), F32)]
    return pl.pallas_call(
        functools.partial(_mlstm_kernel, tile=tile, chunk=chunk, tiles_per_seq=tiles_per_seq),
        grid=(n_tiles + 1,),
        in_specs=[pl.BlockSpec(block, ahead), pl.BlockSpec(block, behind),
                  _const_spec((1, D_MODEL)), _const_spec((1, D_MODEL)),
                  _layer_spec(w_in, layer), _const_spec((ML_QK + 2 * H, D_MODEL)),
                  _const_spec((2 * H, 1)),
                  _const_spec((1, ML_V)), _layer_spec(w_out, layer)],
        out_specs=pl.BlockSpec(block, behind),
        out_shape=jax.ShapeDtypeStruct((B, S, D_MODEL), F32),
        scratch_shapes=[pltpu.VMEM((H, ML_QK_DIM, 2 * ML_V_DIM), F32),
                        pltpu.VMEM((H, 1), F32),
                        pltpu.VMEM((tile, ML_V), F32)] + stage_bufs + stage_bufs,
        compiler_params=pltpu.CompilerParams(
            dimension_semantics=("arbitrary",), vmem_limit_bytes=VMEM_LIMIT_BYTES),
        name="mlstm",
    )(h, h, g_pre.reshape(1, D_MODEL), g_post.reshape(1, D_MODEL), w_in, w_kg,
      b_row, head_norm.reshape(1, ML_V), w_out)


def _lru_kernel(h_ref, gpre_ref, gpost_ref, win_ref, convw_ref, convb_ref, wgate_ref, ba_ref,
                bx_ref, lam_ref, wout_ref, o_ref, uext_ref, hc_ref, *, steps, batch):
    W = LRU_WIDTH
    rows = steps * batch
    halo = (CONV_WIDTH - 1) * batch

    @pl.when(pl.program_id(0) == 0)
    def _():
        uext_ref[0:halo, :] = jnp.zeros((halo, W), F32)
        hc_ref[...] = jnp.zeros_like(hc_ref)

    def time_major(v):
        return jnp.swapaxes(v.reshape(batch, steps, W), 0, 1).reshape(rows, W)

    def batch_major(v):
        return jnp.swapaxes(v.reshape(steps, batch, W), 0, 1).reshape(rows, W)

    h = h_ref[...].reshape(rows, D_MODEL)
    xn = _rms(h, gpre_ref[...]).astype(BF16)
    proj = _dot(xn, win_ref[...])
    gate_branch = proj[:, :W]
    u = time_major(proj[:, W:])

    uext_ref[halo:halo + rows, :] = u
    uc = convw_ref[CONV_WIDTH - 1:CONV_WIDTH, :] * u + convb_ref[...]
    for j in range(1, CONV_WIDTH):
        start = halo - j * batch
        uc = uc + convw_ref[CONV_WIDTH - 1 - j:CONV_WIDTH - j, :] * uext_ref[start:start + rows, :]
    uext_ref[0:halo, :] = u[rows - halo:rows, :]

    ucb = uc.astype(BF16)
    gates = [_dot(ucb[:, n * LRU_BLOCK:(n + 1) * LRU_BLOCK], wgate_ref[n]) for n in range(LRU_BLOCKS)]
    ga = jnp.concatenate([g[:, :LRU_BLOCK] for g in gates], axis=1)
    gx = jnp.concatenate([g[:, LRU_BLOCK:] for g in gates], axis=1)
    tr = jnp.tanh(ga + ba_ref[...])
    ti = jnp.tanh(gx + bx_ref[...])
    half_c = (-0.5 * LRU_C) * jax.nn.log_sigmoid(lam_ref[...])
    neg_log_a = tr * half_c + half_c
    a = jnp.exp2(neg_log_a * -LOG2E)
    gain2 = jnp.tanh(neg_log_a) * (a * a + 1.0)
    gain = gain2 * lax.rsqrt(jnp.maximum(gain2, jnp.finfo(F32).tiny))
    b2 = gain * uc * (ti + 1.0)

    carry = hc_ref[...]
    hs = []
    for t in range(steps):
        carry = a[t * batch:(t + 1) * batch, :] * carry + b2[t * batch:(t + 1) * batch, :]
        hs.append(carry)
    hc_ref[...] = carry
    hseq2 = batch_major(jnp.concatenate(hs, axis=0))

    x = gate_branch
    inner = x * (GELU_K1 * (x * x) + GELU_K0)
    y = (hseq2 * (0.25 * x) * (jnp.tanh(inner) + 1.0)).astype(BF16)
    out = h + _rms(_dot(y, wout_ref[...]), gpost_ref[...])
    o_ref[...] = out.reshape(batch, steps, D_MODEL)


def _lru_layer(h, g_pre, g_post, w_in, conv_w, conv_b, w_gate_a, b_gate_a, w_gate_x, b_gate_x,
               lam, w_out):
    B, S, _ = h.shape
    W = LRU_WIDTH
    steps = min(LRU_ROWS // B, S)
    halo = (CONV_WIDTH - 1) * B
    w_gate = (0.5 * jnp.concatenate([w_gate_a, w_gate_x], axis=-1)).astype(BF16)
    tok_spec = pl.BlockSpec((B, steps, D_MODEL), lambda i: (0, i, 0))
    row = lambda v: v.astype(F32).reshape(1, W)
    return pl.pallas_call(
        functools.partial(_lru_kernel, steps=steps, batch=B),
        grid=(S // steps,),
        in_specs=[tok_spec, _const_spec((1, D_MODEL)), _const_spec((1, D_MODEL)),
                  _const_spec((D_MODEL, 2 * W)), _const_spec((CONV_WIDTH, W)), _const_spec((1, W)),
                  _const_spec((LRU_BLOCKS, LRU_BLOCK, 2 * LRU_BLOCK)), _const_spec((1, W)),
                  _const_spec((1, W)), _const_spec((1, W)), _const_spec((W, D_MODEL))],
        out_specs=tok_spec,
        out_shape=jax.ShapeDtypeStruct((B, S, D_MODEL), F32),
        scratch_shapes=[pltpu.VMEM((steps * B + halo, W), F32), pltpu.VMEM((B, W), F32)],
        compiler_params=pltpu.CompilerParams(
            dimension_semantics=("arbitrary",), vmem_limit_bytes=VMEM_LIMIT_BYTES),
        name="rglru",
    )(h, g_pre.reshape(1, D_MODEL), g_post.reshape(1, D_MODEL),
      w_in.astype(BF16), conv_w.astype(F32), row(conv_b), w_gate, row(0.5 * b_gate_a),
      row(0.5 * b_gate_x), row(lam), w_out.astype(BF16))


def kernel(x, ml_w_in, ml_b_if, ml_head_norm, ml_w_out, lru_w_in, lru_conv_w, lru_conv_b, lru_w_gate_a, lru_b_gate_a, lru_w_gate_x, lru_b_gate_x, lru_lambda, lru_w_out, norm_pre_mix, norm_post_mix, norm_pre_ffn, norm_post_ffn, ffn_w_gate, ffn_w_up, ffn_w_down):
    B, S, D = x.shape
    depth = norm_pre_mix.shape[0]
    h = x
    for layer in range(depth):
        j = layer // 2
        if layer % 2 == 0:
            h = _mlstm_layer(h, j, norm_pre_mix[layer], norm_post_mix[layer], ml_w_in, ml_b_if[j],
                             ml_head_norm[j], ml_w_out)
        else:
            h = _lru_layer(h, norm_pre_mix[layer], norm_post_mix[layer], lru_w_in[j], lru_conv_w[j],
                           lru_conv_b[j], lru_w_gate_a[j], lru_b_gate_a[j], lru_w_gate_x[j],
                           lru_b_gate_x[j], lru_lambda[j], lru_w_out[j])
        h = _ffn_layer(h.reshape(B * S, D), layer, norm_pre_ffn[layer], norm_post_ffn[layer],
                       ffn_w_gate, ffn_w_up, ffn_w_down).reshape(B, S, D)
    return h
```

```python
import functools
import math

import jax
import jax.numpy as jnp
from jax import lax
from jax.experimental import pallas as pl
from jax.experimental.pallas import tpu as pltpu

D_MODEL = 1024
ML_HEADS = 8
ML_QK_DIM = 64
ML_V_DIM = 128
ML_QK = ML_HEADS * ML_QK_DIM
ML_V = ML_HEADS * ML_V_DIM
GATE_CAP = 15.0
ML_M_INIT = -1e30
LRU_WIDTH = D_MODEL
LRU_BLOCKS = 4
LRU_BLOCK = LRU_WIDTH // LRU_BLOCKS
CONV_WIDTH = 4
LRU_C = 8.0
D_FF = 2816
EPS = 1e-6
LOG2E = math.log2(math.e)
GELU_K0 = math.sqrt(2.0 / math.pi)
GELU_K1 = GELU_K0 * 0.044715

SUBLANES = 8
LANES = 128
VMEM_LIMIT_BYTES = 56 * 1024 * 1024

ML_CHUNK = 128
ML_TILE = 512
LRU_ROWS = 1024
FFN_TILE = 512
FFN_CHUNK = 256

F32 = jnp.float32
BF16 = jnp.bfloat16


def _rms(x, g):
    return x * lax.rsqrt(jnp.mean(x * x, axis=-1, keepdims=True) + EPS) * g


def _sigmoid(x):
    return 0.5 * jnp.tanh(0.5 * x) + 0.5


def _dot(a, b):
    return jnp.dot(a, b, preferred_element_type=F32)


def _dot_nt(a, b):
    return lax.dot_general(a, b, (((1,), (1,)), ((), ())), preferred_element_type=F32)


def _const_spec(shape):
    zeros = (0,) * len(shape)
    return pl.BlockSpec(shape, lambda *_: zeros, pipeline_mode=pl.Buffered(1))


def _layer_spec(stacked, layer):
    shape = stacked.shape[1:]
    zeros = (0,) * len(shape)
    return pl.BlockSpec((None,) + shape, lambda *_: (layer,) + zeros, pipeline_mode=pl.Buffered(1))


def _ffn_kernel(h_ref, gpre_ref, gpost_ref, wg_ref, wu_ref, wd_ref, o_ref, *, fc):
    h = h_ref[...]
    xn = _rms(h, gpre_ref[...])
    acc = None
    for c in range(D_FF // fc):
        cols = slice(c * fc, (c + 1) * fc)
        half_g = 0.5 * _dot(xn, wg_ref[:, cols])
        a = (jnp.tanh(half_g) * half_g + half_g) * _dot(xn, wu_ref[:, cols])
        part = _dot(a, wd_ref[cols, :])
        acc = part if acc is None else acc + part
    o_ref[...] = h + _rms(acc, gpost_ref[...])


def _ffn_layer(h, layer, g_pre, g_post, w_gate, w_up, w_down):
    n_tok = h.shape[0]
    tile = min(FFN_TILE, n_tok)
    tok_spec = pl.BlockSpec((tile, D_MODEL), lambda i: (i, 0))
    return pl.pallas_call(
        functools.partial(_ffn_kernel, fc=FFN_CHUNK),
        grid=(n_tok // tile,),
        in_specs=[tok_spec, _const_spec((1, D_MODEL)), _const_spec((1, D_MODEL)),
                  _layer_spec(w_gate, layer), _layer_spec(w_up, layer), _layer_spec(w_down, layer)],
        out_specs=tok_spec,
        out_shape=jax.ShapeDtypeStruct((n_tok, D_MODEL), F32),
        compiler_params=pltpu.CompilerParams(
            dimension_semantics=("parallel",), vmem_limit_bytes=VMEM_LIMIT_BYTES),
        name="swiglu",
    )(h, g_pre.reshape(1, D_MODEL), g_post.reshape(1, D_MODEL), w_gate, w_up, w_down)


def _soft_cap(z):
    return GATE_CAP * jnp.tanh(z / GATE_CAP)


def _lane_prefix(x, combine, identity):
    n = x.shape[-1]
    lane = lax.broadcasted_iota(jnp.int32, x.shape, x.ndim - 1)
    d = 1
    while d < n:
        x = combine(x, jnp.where(lane >= d, pltpu.roll(x, d, axis=x.ndim - 1), identity))
        d *= 2
    return x


def _mlstm_step(hx_ref, hy_ref, gpre_ref, gpost_ref, win_ref, wkg_ref, brow_ref, hnorm_ref, wout_ref,
                write, read, o_ref, c_ref, m_ref, hs_ref, *, tile, chunk):
    H, dk, dv, L = ML_HEADS, ML_QK_DIM, ML_V_DIM, chunk
    n_chunks = tile // L
    log2_q_scale = -0.5 * math.log2(dk)
    q_w, v_w, gate_w, kt_w, prt_w, rr2_w, ea_w, dec_w = write
    q_r, v_r, gate_r, kt_r, prt_r, rr2_r, ea_r, dec_r = read

    causal = (lax.broadcasted_iota(jnp.int32, (L, L), 1)
              <= lax.broadcasted_iota(jnp.int32, (L, L), 0))
    ones_aug = jnp.ones((L, dv), F32)
    zeros_kk = jnp.zeros((dk, dk), F32)

    def scores(c):
        r0 = c * L
        return [_dot(q_r[hh, r0:r0 + L, :], kt_r[hh * dk:(hh + 1) * dk, r0:r0 + L]) for hh in range(H)]

    def state_matmul(c, hh, s_mat):
        r0 = c * L
        q_h = q_r[hh, r0:r0 + L, :]
        kt_h = kt_r[hh * dk:(hh + 1) * dk, r0:r0 + L]
        v_h = v_r[r0:r0 + L, hh * dv:(hh + 1) * dv]
        cm2 = prt_r[c, :, hh:hh + 1]
        w_inter = prt_r[c, :, H + hh:H + hh + 1]
        c_prev = c_ref[hh]
        p_mat = jnp.where(causal, jnp.exp2(cm2 + rr2_r[hh:hh + 1, r0:r0 + L]) * s_mat, 0.0)
        lhs = jnp.concatenate(
            [jnp.concatenate([p_mat, q_h * w_inter], axis=1),
             jnp.concatenate([kt_h * ea_r[hh:hh + 1, r0:r0 + L], zeros_kk], axis=1)], axis=0)
        rhs = jnp.concatenate([jnp.concatenate([v_h, ones_aug], axis=1), c_prev], axis=0)
        r = _dot(lhs, rhs)
        c_ref[hh] = dec_r[c, hh:hh + 1, :] * c_prev + r[L:L + dk]
        return r

    def epilogue(c, hh, r):
        r0 = c * L
        emt = prt_r[c, :, 2 * H + hh:2 * H + hh + 1]
        den = jnp.maximum(jnp.abs(r[0:L, dv:2 * dv]), emt)
        h_out = r[0:L, 0:dv] / den
        hn = h_out * lax.rsqrt(jnp.mean(h_out * h_out, axis=-1, keepdims=True) + EPS)
        hs_ref[r0:r0 + L, hh * dv:(hh + 1) * dv] = hn * gate_r[r0:r0 + L, hh * dv:(hh + 1) * dv]

    PW = 2 * LANES
    gate_rows = []

    def kg_piece(t0):
        def run():
            kg = _dot_nt(wkg_ref[...], xn[t0:t0 + PW, :])
            kt_w[:, t0:t0 + PW] = kg[0:ML_QK, :]
            gate_rows.append(_soft_cap(kg[ML_QK:ML_QK + 2 * H, :] + brow_ref[...]))
        return run

    def q_piece(c0):
        def run():
            q4 = _dot(xn, win_ref[:, c0:c0 + PW])
            for j in range(PW // dk):
                q_w[c0 // dk + j] = q4[:, j * dk:(j + 1) * dk]
        return run

    def v_piece(c0):
        def run():
            v_w[:, c0:c0 + PW] = _dot(xn, win_ref[:, 2 * ML_QK + c0:2 * ML_QK + c0 + PW])
        return run

    def gate_piece(c0):
        def run():
            o = _dot(xn, win_ref[:, 2 * ML_QK + ML_V + c0:2 * ML_QK + ML_V + c0 + PW])
            gate_w[:, c0:c0 + PW] = _sigmoid(o) * hnorm_ref[:, c0:c0 + PW]
        return run

    def bookkeeping():
        grow = jnp.concatenate(gate_rows, axis=1)
        li_all = grow[0:H, :]
        lf_all = jax.nn.log_sigmoid(grow[H:2 * H, :])
        m_prev = m_ref[...]
        for c in range(n_chunks):
            li = li_all[:, c * L:(c + 1) * L]
            b = _lane_prefix(lf_all[:, c * L:(c + 1) * L], jnp.add, 0.0)
            rr = li - b
            rr_max = _lane_prefix(rr, jnp.maximum, -jnp.inf)
            g_tot = b[:, L - 1:L]
            m_t = b + jnp.maximum(rr_max, m_prev)
            m_new = jnp.maximum(g_tot + m_prev, g_tot + rr_max[:, L - 1:L])
            dec_w[c] = jnp.exp(g_tot + m_prev - m_new)
            ea_w[:, c * L:(c + 1) * L] = jnp.exp(g_tot + rr - m_new)
            rr2_w[:, c * L:(c + 1) * L] = rr * LOG2E
            per_row = jnp.concatenate(
                [(b - m_t) * LOG2E + log2_q_scale, jnp.exp(b + m_prev - m_t) * 2.0 ** log2_q_scale,
                 jnp.exp(-m_t), jnp.zeros((L - 3 * H, L), F32)], axis=0)
            prt_w[c] = per_row.T
            m_prev = m_new
        m_ref[...] = m_prev

    pieces = ([kg_piece(t0) for t0 in range(0, tile, PW)] + [q_piece(c0) for c0 in range(0, ML_QK, PW)]
              + [v_piece(c0) for c0 in range(0, ML_V, PW)] + [gate_piece(c0) for c0 in range(0, ML_V, PW)])
    slots = n_chunks * 3
    per_slot = -(-len(pieces) // slots)

    def issue_pieces():
        for _ in range(per_slot):
            if pieces:
                pieces.pop(0)()

    s_mat = scores(0)
    xn = _rms(hx_ref[...], gpre_ref[...])
    for c in range(n_chunks):
        res = {}
        for hh in range(H):
            res[hh] = state_matmul(c, hh, s_mat[hh])
            if hh in (1, 4, 7):
                issue_pieces()
                for done in (hh - 2, hh - 1, hh) if hh > 1 else (0, 1):
                    epilogue(c, done, res[done])
        if c + 1 < n_chunks:
            s_mat = scores(c + 1)
    while pieces:
        pieces.pop(0)()
    bookkeeping()

    y = _dot(hs_ref[...], wout_ref[...])
    o_ref[...] = hy_ref[...] + _rms(y, gpost_ref[...])


def _mlstm_kernel(hx_ref, hy_ref, gpre_ref, gpost_ref, win_ref, wkg_ref, brow_ref, hnorm_ref,
                  wout_ref, o_ref, c_ref, m_ref, hs_ref, *proj_refs, tile, chunk, tiles_per_seq):
    s = pl.program_id(0)
    n_buf = len(proj_refs) // 2
    bufs = (proj_refs[:n_buf], proj_refs[n_buf:])

    @pl.when(s == 0)
    def _():
        for ref in bufs[1]:
            ref[...] = jnp.zeros_like(ref)

    @pl.when(s % tiles_per_seq == 0)
    def _():
        m_ref[...] = jnp.full_like(m_ref, ML_M_INIT)

    @pl.when(jnp.maximum(s - 1, 0) % tiles_per_seq == 0)
    def _():
        c_ref[...] = jnp.zeros_like(c_ref)

    step = functools.partial(
        _mlstm_step, hx_ref, hy_ref, gpre_ref, gpost_ref, win_ref, wkg_ref, brow_ref, hnorm_ref,
        wout_ref, o_ref=o_ref, c_ref=c_ref, m_ref=m_ref, hs_ref=hs_ref, tile=tile, chunk=chunk)
    pl.when(s % 2 == 0)(lambda: step(bufs[0], bufs[1]))
    pl.when(s % 2 == 1)(lambda: step(bufs[1], bufs[0]))


def _mlstm_layer(h, layer, g_pre, g_post, w_in, b_if, head_norm, w_out):
    B, S, _ = h.shape
    H = ML_HEADS
    tile = min(ML_TILE, S)
    chunk = min(ML_CHUNK, tile)
    tiles_per_seq = S // tile
    n_tiles = B * tiles_per_seq
    w_kg = jnp.concatenate([w_in[layer, :, ML_QK:2 * ML_QK], w_in[layer, :, 2 * ML_QK + 2 * ML_V:]], axis=1).T
    b_row = b_if.astype(F32).reshape(2 * H, 1)

    def ahead(s):
        k = jnp.minimum(s, n_tiles - 1)
        return (k // tiles_per_seq, k % tiles_per_seq, 0)

    def behind(s):
        k = jnp.maximum(s - 1, 0)
        return (k // tiles_per_seq, k % tiles_per_seq, 0)

    block = (None, tile, D_MODEL)
    n_chunks = tile // chunk
    stage_bufs = [pltpu.VMEM((H, tile, ML_QK_DIM), F32),
                  pltpu.VMEM((tile, ML_V), F32),
                  pltpu.VMEM((tile, ML_V), F32),
                  pltpu.VMEM((ML_QK, tile), F32),
                  pltpu.VMEM((n_chunks, chunk, LANES), F32),
                  pltpu.VMEM((H, tile), F32),
                  pltpu.VMEM((H, tile), F32),
                  pltpu.VMEM((n_chunks, H, 1), F32)]
    return pl.pallas_call(
        functools.partial(_mlstm_kernel, tile=tile, chunk=chunk, tiles_per_seq=tiles_per_seq),
        grid=(n_tiles + 1,),
        in_specs=[pl.BlockSpec(block, ahead), pl.BlockSpec(block, behind),
                  _const_spec((1, D_MODEL)), _const_spec((1, D_MODEL)),
                  _layer_spec(w_in, layer), _const_spec((ML_QK + 2 * H, D_MODEL)),
                  _const_spec((2 * H, 1)),
                  _const_spec((1, ML_V)), _layer_spec(w_out, layer)],
        out_specs=pl.BlockSpec(block, behind),
        out_shape=jax.ShapeDtypeStruct((B, S, D_MODEL), F32),
        scratch_shapes=[pltpu.VMEM((H, ML_QK_DIM, 2 * ML_V_DIM), F32),
                        pltpu.VMEM((H, 1), F32),
                        pltpu.VMEM((tile, ML_V), F32)] + stage_bufs + stage_bufs,
        compiler_params=pltpu.CompilerParams(
            dimension_semantics=("arbitrary",), vmem_limit_bytes=VMEM_LIMIT_BYTES),
        name="mlstm",
    )(h, h, g_pre.reshape(1, D_MODEL), g_post.reshape(1, D_MODEL), w_in, w_kg,
      b_row, head_norm.reshape(1, ML_V), w_out)


def _lru_kernel(h_ref, gpre_ref, gpost_ref, win_ref, convw_ref, convb_ref, wgate_ref, ba_ref,
                bx_ref, lam_ref, wout_ref, o_ref, uext_ref, hc_ref, *, steps, batch):
    W = LRU_WIDTH
    rows = steps * batch
    halo = (CONV_WIDTH - 1) * batch

    @pl.when(pl.program_id(0) == 0)
    def _():
        uext_ref[0:halo, :] = jnp.zeros((halo, W), F32)
        hc_ref[...] = jnp.zeros_like(hc_ref)

    def time_major(v):
        return jnp.swapaxes(v.reshape(batch, steps, W), 0, 1).reshape(rows, W)

    def batch_major(v):
        return jnp.swapaxes(v.reshape(steps, batch, W), 0, 1).reshape(rows, W)

    h = h_ref[...].reshape(rows, D_MODEL)
    xn = _rms(h, gpre_ref[...]).astype(BF16)
    proj = _dot(xn, win_ref[...])
    gate_branch = proj[:, :W]
    u = time_major(proj[:, W:])

    uext_ref[halo:halo + rows, :] = u
    uc = convw_ref[CONV_WIDTH - 1:CONV_WIDTH, :] * u + convb_ref[...]
    for j in range(1, CONV_WIDTH):
        start = halo - j * batch
        uc = uc + convw_ref[CONV_WIDTH - 1 - j:CONV_WIDTH - j, :] * uext_ref[start:start + rows, :]
    uext_ref[0:halo, :] = u[rows - halo:rows, :]

    ucb = uc.astype(BF16)
    gates = [_dot(ucb[:, n * LRU_BLOCK:(n + 1) * LRU_BLOCK], wgate_ref[n]) for n in range(LRU_BLOCKS)]
    ga = jnp.concatenate([g[:, :LRU_BLOCK] for g in gates], axis=1)
    gx = jnp.concatenate([g[:, LRU_BLOCK:] for g in gates], axis=1)
    tr = jnp.tanh(ga + ba_ref[...])
    ti = jnp.tanh(gx + bx_ref[...])
    half_c = (-0.5 * LRU_C) * jax.nn.log_sigmoid(lam_ref[...])
    neg_log_a = tr * half_c + half_c
    a = jnp.exp2(neg_log_a * -LOG2E)
    gain2 = jnp.tanh(neg_log_a) * (a * a + 1.0)
    gain = gain2 * lax.rsqrt(jnp.maximum(gain2, jnp.finfo(F32).tiny))
    b2 = gain * uc * (ti + 1.0)

    carry = hc_ref[...]
    hs = []
    for t in range(steps):
        carry = a[t * batch:(t + 1) * batch, :] * carry + b2[t * batch:(t + 1) * batch, :]
        hs.append(carry)
    hc_ref[...] = carry
    hseq2 = batch_major(jnp.concatenate(hs, axis=0))

    x = gate_branch
    inner = x * (GELU_K1 * (x * x) + GELU_K0)
    y = (hseq2 * (0.25 * x) * (jnp.tanh(inner) + 1.0)).astype(BF16)
    out = h + _rms(_dot(y, wout_ref[...]), gpost_ref[...])
    o_ref[...] = out.reshape(batch, steps, D_MODEL)


def _lru_layer(h, g_pre, g_post, w_in, conv_w, conv_b, w_gate_a, b_gate_a, w_gate_x, b_gate_x,
               lam, w_out):
    B, S, _ = h.shape
    W = LRU_WIDTH
    steps = min(LRU_ROWS // B, S)
    halo = (CONV_WIDTH - 1) * B
    w_gate = (0.5 * jnp.concatenate([w_gate_a, w_gate_x], axis=-1)).astype(BF16)
    tok_spec = pl.BlockSpec((B, steps, D_MODEL), lambda i: (0, i, 0))
    row = lambda v: v.astype(F32).reshape(1, W)
    return pl.pallas_call(
        functools.partial(_lru_kernel, steps=steps, batch=B),
        grid=(S // steps,),
        in_specs=[tok_spec, _const_spec((1, D_MODEL)), _const_spec((1, D_MODEL)),
                  _const_spec((D_MODEL, 2 * W)), _const_spec((CONV_WIDTH, W)), _const_spec((1, W)),
                  _const_spec((LRU_BLOCKS, LRU_BLOCK, 2 * LRU_BLOCK)), _const_spec((1, W)),
                  _const_spec((1, W)), _const_spec((1, W)), _const_spec((W, D_MODEL))],
        out_specs=tok_spec,
        out_shape=jax.ShapeDtypeStruct((B, S, D_MODEL), F32),
        scratch_shapes=[pltpu.VMEM((steps * B + halo, W), F32), pltpu.VMEM((B, W), F32)],
        compiler_params=pltpu.CompilerParams(
            dimension_semantics=("arbitrary",), vmem_limit_bytes=VMEM_LIMIT_BYTES),
        name="rglru",
    )(h, g_pre.reshape(1, D_MODEL), g_post.reshape(1, D_MODEL),
      w_in.astype(BF16), conv_w.astype(F32), row(conv_b), w_gate, row(0.5 * b_gate_a),
      row(0.5 * b_gate_x), row(lam), w_out.astype(BF16))


def kernel(x, ml_w_in, ml_b_if, ml_head_norm, ml_w_out, lru_w_in, lru_conv_w, lru_conv_b, lru_w_gate_a, lru_b_gate_a, lru_w_gate_x, lru_b_gate_x, lru_lambda, lru_w_out, norm_pre_mix, norm_post_mix, norm_pre_ffn, norm_post_ffn, ffn_w_gate, ffn_w_up, ffn_w_down):
    B, S, D = x.shape
    depth = norm_pre_mix.shape[0]
    h = x
    for layer in range(depth):
        j = layer // 2
        if layer % 2 == 0:
            h = _mlstm_layer(h, j, norm_pre_mix[layer], norm_post_mix[layer], ml_w_in, ml_b_if[j],
                             ml_head_norm[j], ml_w_out)
        else:
            h = _lru_layer(h, norm_pre_mix[layer], norm_post_mix[layer], lru_w_in[j], lru_conv_w[j],
                           lru_conv_b[j], lru_w_gate_a[j], lru_b_gate_a[j], lru_w_gate_x[j],
                           lru_b_gate_x[j], lru_lambda[j], lru_w_out[j])
        h = _ffn_layer(h.reshape(B * S, D), layer, norm_pre_ffn[layer], norm_post_ffn[layer],
                       ffn_w_gate, ffn_w_up, ffn_w_down).reshape(B, S, D)
    return h
```

```python
import functools
import math

import jax
import jax.numpy as jnp
from jax import lax
from jax.experimental import pallas as pl
from jax.experimental.pallas import tpu as pltpu

D_MODEL = 1024
ML_HEADS = 8
ML_QK_DIM = 64
ML_V_DIM = 128
ML_QK = ML_HEADS * ML_QK_DIM
ML_V = ML_HEADS * ML_V_DIM
GATE_CAP = 15.0
ML_M_INIT = -1e30
LRU_WIDTH = D_MODEL
LRU_BLOCKS = 4
LRU_BLOCK = LRU_WIDTH // LRU_BLOCKS
CONV_WIDTH = 4
LRU_C = 8.0
D_FF = 2816
EPS = 1e-6
LOG2E = math.log2(math.e)
GELU_K0 = math.sqrt(2.0 / math.pi)
GELU_K1 = GELU_K0 * 0.044715

SUBLANES = 8
LANES = 128
VMEM_LIMIT_BYTES = 56 * 1024 * 1024

ML_CHUNK = 128
ML_TILE = 512
LRU_ROWS = 1024
FFN_TILE = 512
FFN_CHUNK = 256

F32 = jnp.float32
BF16 = jnp.bfloat16


def _rms(x, g):
    return x * lax.rsqrt(jnp.mean(x * x, axis=-1, keepdims=True) + EPS) * g


def _sigmoid(x):
    return 0.5 * jnp.tanh(0.5 * x) + 0.5


def _dot(a, b):
    return jnp.dot(a, b, preferred_element_type=F32)


def _dot_nt(a, b):
    return lax.dot_general(a, b, (((1,), (1,)), ((), ())), preferred_element_type=F32)


def _const_spec(shape):
    zeros = (0,) * len(shape)
    return pl.BlockSpec(shape, lambda *_: zeros, pipeline_mode=pl.Buffered(1))


def _layer_spec(stacked, layer):
    shape = stacked.shape[1:]
    zeros = (0,) * len(shape)
    return pl.BlockSpec((None,) + shape, lambda *_: (layer,) + zeros, pipeline_mode=pl.Buffered(1))


def _ffn_kernel(h_ref, gpre_ref, gpost_ref, wg_ref, wu_ref, wd_ref, o_ref, *, fc):
    h = h_ref[...]
    xn = _rms(h, gpre_ref[...])
    acc = None
    for c in range(D_FF // fc):
        cols = slice(c * fc, (c + 1) * fc)
        half_g = 0.5 * _dot(xn, wg_ref[:, cols])
        a = (jnp.tanh(half_g) * half_g + half_g) * _dot(xn, wu_ref[:, cols])
        part = _dot(a, wd_ref[cols, :])
        acc = part if acc is None else acc + part
    o_ref[...] = h + _rms(acc, gpost_ref[...])


def _ffn_layer(h, layer, g_pre, g_post, w_gate, w_up, w_down):
    n_tok = h.shape[0]
    tile = min(FFN_TILE, n_tok)
    tok_spec = pl.BlockSpec((tile, D_MODEL), lambda i: (i, 0))
    return pl.pallas_call(
        functools.partial(_ffn_kernel, fc=FFN_CHUNK),
        grid=(n_tok // tile,),
        in_specs=[tok_spec, _const_spec((1, D_MODEL)), _const_spec((1, D_MODEL)),
                  _layer_spec(w_gate, layer), _layer_spec(w_up, layer), _layer_spec(w_down, layer)],
        out_specs=tok_spec,
        out_shape=jax.ShapeDtypeStruct((n_tok, D_MODEL), F32),
        compiler_params=pltpu.CompilerParams(
            dimension_semantics=("parallel",), vmem_limit_bytes=VMEM_LIMIT_BYTES),
        name="swiglu",
    )(h, g_pre.reshape(1, D_MODEL), g_post.reshape(1, D_MODEL), w_gate, w_up, w_down)


def _soft_cap(z):
    return GATE_CAP * jnp.tanh(z / GATE_CAP)


def _lane_prefix(x, combine, identity):
    n = x.shape[-1]
    lane = lax.broadcasted_iota(jnp.int32, x.shape, x.ndim - 1)
    d = 1
    while d < n:
        x = combine(x, jnp.where(lane >= d, pltpu.roll(x, d, axis=x.ndim - 1), identity))
        d *= 2
    return x


def _mlstm_step(hx_ref, hy_ref, gpre_ref, gpost_ref, win_ref, wkg_ref, brow_ref, hnorm_ref, wout_ref,
                write, read, o_ref, c_ref, m_ref, hs_ref, *, tile, chunk):
    H, dk, dv, L = ML_HEADS, ML_QK_DIM, ML_V_DIM, chunk
    n_chunks = tile // L
    log2_q_scale = -0.5 * math.log2(dk)
    q_w, v_w, gate_w, kt_w, prt_w, rr2_w, ea_w, dec_w = write
    q_r, v_r, gate_r, kt_r, prt_r, rr2_r, ea_r, dec_r = read

    causal = (lax.broadcasted_iota(jnp.int32, (L, L), 1)
              <= lax.broadcasted_iota(jnp.int32, (L, L), 0))
    ones_aug = jnp.ones((L, dv), F32)
    zeros_kk = jnp.zeros((dk, dk), F32)

    def scores(c):
        r0 = c * L
        return [_dot(q_r[hh, r0:r0 + L, :], kt_r[hh * dk:(hh + 1) * dk, r0:r0 + L]) for hh in range(H)]

    def state_matmul(c, hh, s_mat):
        r0 = c * L
        q_h = q_r[hh, r0:r0 + L, :]
        kt_h = kt_r[hh * dk:(hh + 1) * dk, r0:r0 + L]
        v_h = v_r[r0:r0 + L, hh * dv:(hh + 1) * dv]
        cm2 = prt_r[c, :, hh:hh + 1]
        w_inter = prt_r[c, :, H + hh:H + hh + 1]
        c_prev = c_ref[hh]
        p_mat = jnp.where(causal, jnp.exp2(cm2 + rr2_r[hh:hh + 1, r0:r0 + L]) * s_mat, 0.0)
        lhs = jnp.concatenate(
            [jnp.concatenate([p_mat, q_h * w_inter], axis=1),
             jnp.concatenate([kt_h * ea_r[hh:hh + 1, r0:r0 + L], zeros_kk], axis=1)], axis=0)
        rhs = jnp.concatenate([jnp.concatenate([v_h, ones_aug], axis=1), c_prev], axis=0)
        r = _dot(lhs, rhs)
        c_ref[hh] = dec_r[c, hh:hh + 1, :] * c_prev + r[L:L + dk]
        return r

    def epilogue(c, hh, r):
        r0 = c * L
        emt = prt_r[c, :, 2 * H + hh:2 * H + hh + 1]
        den = jnp.maximum(jnp.abs(r[0:L, dv:2 * dv]), emt)
        h_out = r[0:L, 0:dv] / den
        hn = h_out * lax.rsqrt(jnp.mean(h_out * h_out, axis=-1, keepdims=True) + EPS)
        hs_ref[r0:r0 + L, hh * dv:(hh + 1) * dv] = hn * gate_r[r0:r0 + L, hh * dv:(hh + 1) * dv]

    PW = 2 * LANES
    gate_rows = []

    def kg_piece(t0):
        def run():
            kg = _dot_nt(wkg_ref[...], xn[t0:t0 + PW, :])
            kt_w[:, t0:t0 + PW] = kg[0:ML_QK, :]
            gate_rows.append(_soft_cap(kg[ML_QK:ML_QK + 2 * H, :] + brow_ref[...]))
        return run

    def q_piece(c0):
        def run():
            q4 = _dot(xn, win_ref[:, c0:c0 + PW])
            for j in range(PW // dk):
                q_w[c0 // dk + j] = q4[:, j * dk:(j + 1) * dk]
        return run

    def v_piece(c0):
        def run():
            v_w[:, c0:c0 + PW] = _dot(xn, win_ref[:, 2 * ML_QK + c0:2 * ML_QK + c0 + PW])
        return run

    def gate_piece(c0):
        def run():
            o = _dot(xn, win_ref[:, 2 * ML_QK + ML_V + c0:2 * ML_QK + ML_V + c0 + PW])
            gate_w[:, c0:c0 + PW] = _sigmoid(o) * hnorm_ref[:, c0:c0 + PW]
        return run

    def bookkeeping():
        grow = jnp.concatenate(gate_rows, axis=1)
        li_all = grow[0:H, :]
        lf_all = jax.nn.log_sigmoid(grow[H:2 * H, :])
        m_prev = m_ref[...]
        for c in range(n_chunks):
            li = li_all[:, c * L:(c + 1) * L]
            b = _lane_prefix(lf_all[:, c * L:(c + 1) * L], jnp.add, 0.0)
            rr = li - b
            rr_max = _lane_prefix(rr, jnp.maximum, -jnp.inf)
            g_tot = b[:, L - 1:L]
            m_t = b + jnp.maximum(rr_max, m_prev)
            m_new = jnp.maximum(g_tot + m_prev, g_tot + rr_max[:, L - 1:L])
            dec_w[c] = jnp.exp(g_tot + m_prev - m_new)
            ea_w[:, c * L:(c + 1) * L] = jnp.exp(g_tot + rr - m_new)
            rr2_w[:, c * L:(c + 1) * L] = rr * LOG2E
            per_row = jnp.concatenate(
                [(b - m_t) * LOG2E + log2_q_scale, jnp.exp(b + m_prev - m_t) * 2.0 ** log2_q_scale,
                 jnp.exp(-m_t), jnp.zeros((L - 3 * H, L), F32)], axis=0)
            prt_w[c] = per_row.T
            m_prev = m_new
        m_ref[...] = m_prev

    pieces = ([kg_piece(t0) for t0 in range(0, tile, PW)] + [q_piece(c0) for c0 in range(0, ML_QK, PW)]
              + [v_piece(c0) for c0 in range(0, ML_V, PW)] + [gate_piece(c0) for c0 in range(0, ML_V, PW)])
    slots = n_chunks * 3
    per_slot = -(-len(pieces) // slots)

    def issue_pieces():
        for _ in range(per_slot):
            if pieces:
                pieces.pop(0)()

    s_mat = scores(0)
    xn = _rms(hx_ref[...], gpre_ref[...])
    for c in range(n_chunks):
        res = {}
        for hh in range(H):
            res[hh] = state_matmul(c, hh, s_mat[hh])
            if hh in (1, 4, 7):
                issue_pieces()
                for done in (hh - 2, hh - 1, hh) if hh > 1 else (0, 1):
                    epilogue(c, done, res[done])
        if c + 1 < n_chunks:
            s_mat = scores(c + 1)
    while pieces:
        pieces.pop(0)()
    bookkeeping()

    y = _dot(hs_ref[...], wout_ref[...])
    o_ref[...] = hy_ref[...] + _rms(y, gpost_ref[...])


def _mlstm_kernel(hx_ref, hy_ref, gpre_ref, gpost_ref, win_ref, wkg_ref, brow_ref, hnorm_ref,
                  wout_ref, o_ref, c_ref, m_ref, hs_ref, *proj_refs, tile, chunk, tiles_per_seq):
    s = pl.program_id(0)
    n_buf = len(proj_refs) // 2
    bufs = (proj_refs[:n_buf], proj_refs[n_buf:])

    @pl.when(s == 0)
    def _():
        for ref in bufs[1]:
            ref[...] = jnp.zeros_like(ref)

    @pl.when(s % tiles_per_seq == 0)
    def _():
        m_ref[...] = jnp.full_like(m_ref, ML_M_INIT)

    @pl.when(jnp.maximum(s - 1, 0) % tiles_per_seq == 0)
    def _():
        c_ref[...] = jnp.zeros_like(c_ref)

    step = functools.partial(
        _mlstm_step, hx_ref, hy_ref, gpre_ref, gpost_ref, win_ref, wkg_ref, brow_ref, hnorm_ref,
        wout_ref, o_ref=o_ref, c_ref=c_ref, m_ref=m_ref, hs_ref=hs_ref, tile=tile, chunk=chunk)
    pl.when(s % 2 == 0)(lambda: step(bufs[0], bufs[1]))
    pl.when(s % 2 == 1)(lambda: step(bufs[1], bufs[0]))


def _mlstm_layer(h, layer, g_pre, g_post, w_in, b_if, head_norm, w_out):
    B, S, _ = h.shape
    H = ML_HEADS
    tile = min(ML_TILE, S)
    chunk = min(ML_CHUNK, tile)
    tiles_per_seq = S // tile
    n_tiles = B * tiles_per_seq
    w_kg = jnp.concatenate([w_in[layer, :, ML_QK:2 * ML_QK], w_in[layer, :, 2 * ML_QK + 2 * ML_V:]], axis=1).T
    b_row = b_if.astype(F32).reshape(2 * H, 1)

    def ahead(s):
        k = jnp.minimum(s, n_tiles - 1)
        return (k // tiles_per_seq, k % tiles_per_seq, 0)

    def behind(s):
        k = jnp.maximum(s - 1, 0)
        return (k // tiles_per_seq, k % tiles_per_seq, 0)

    block = (None, tile, D_MODEL)
    n_chunks = tile // chunk
    stage_bufs = [pltpu.VMEM((H, tile, ML_QK_DIM), F32),
                  pltpu.VMEM((tile, ML_V), F32),
                  pltpu.VMEM((tile, ML_V), F32),
                  pltpu.VMEM((ML_QK, tile), F32),
                  pltpu.VMEM((n_chunks, chunk, LANES), F32),
                  pltpu.VMEM((H, tile), F32),
                  pltpu.VMEM((H, tile), F32),
                  pltpu.VMEM((n_chunks, H, 1), F32)]
    return pl.pallas_call(
        functools.partial(_mlstm_kernel, tile=tile, chunk=chunk, tiles_per_seq=tiles_per_seq),
        grid=(n_tiles + 1,),
        in_specs=[pl.BlockSpec(block, ahead), pl.BlockSpec(block, behind),
                  _const_spec((1, D_MODEL)), _const_spec((1, D_MODEL)),
                  _layer_spec(w_in, layer), _const_spec((ML_QK + 2 * H, D_MODEL)),
                  _const_spec((2 * H, 1)),
                  _const_spec((1, ML_V)), _layer_spec(w_out, layer)],
        out_specs=pl.BlockSpec(block, behind),
        out_shape=jax.ShapeDtypeStruct((B, S, D_MODEL), F32),
        scratch_shapes=[pltpu.VMEM((H, ML_QK_DIM, 2 * ML_V_DIM), F32),
                        pltpu.VMEM((H, 1), F32),
                        pltpu.VMEM((tile, ML_V), F32)] + stage_bufs + stage_bufs,
        compiler_params=pltpu.CompilerParams(
            dimension_semantics=("arbitrary",), vmem_limit_bytes=VMEM_LIMIT_BYTES),
        name="mlstm",
    )(h, h, g_pre.reshape(1, D_MODEL), g_post.reshape(1, D_MODEL), w_in, w_kg,
      b_row, head_norm.reshape(1, ML_V), w_out)


def _lru_kernel(h_hbm, gpre_ref, gpost_ref, win_ref, convw_ref, convb_ref, wgate_ref, ba_ref,
                bx_ref, lam_ref, wout_ref, o_hbm, in_buf, out_buf, in_sem, out_sem, uext_ref, hc_ref,
                *, steps, batch, n_tiles):
    W = LRU_WIDTH
    rows = steps * batch
    halo = (CONV_WIDTH - 1) * batch
    i = pl.program_id(0)
    slot = lax.rem(i, 2)

    def tile_in(tile, to_slot):
        t0 = pl.multiple_of(tile * steps, steps)
        return [pltpu.make_async_copy(h_hbm.at[b, pl.ds(t0, steps), :], in_buf.at[to_slot, :, b, :],
                                      in_sem.at[to_slot]) for b in range(batch)]

    def tile_out(tile, from_slot):
        t0 = pl.multiple_of(tile * steps, steps)
        return [pltpu.make_async_copy(out_buf.at[from_slot, :, b, :], o_hbm.at[b, pl.ds(t0, steps), :],
                                      out_sem.at[from_slot]) for b in range(batch)]

    def start(copies):
        for c in copies:
            c.start()

    def wait(copies):
        for c in copies:
            c.wait()

    @pl.when(i == 0)
    def _():
        uext_ref[0:halo, :] = jnp.zeros((halo, W), F32)
        hc_ref[...] = jnp.zeros_like(hc_ref)
        start(tile_in(0, 0))

    @pl.when(i + 1 < n_tiles)
    def _():
        start(tile_in(i + 1, 1 - slot))

    wait(tile_in(i, slot))

    @pl.when(i >= 2)
    def _():
        wait(tile_out(i - 2, slot))

    h = in_buf[slot].reshape(rows, D_MODEL)
    xn = _rms(h, gpre_ref[...]).astype(BF16)
    proj = _dot(xn, win_ref[...])
    gate_branch = proj[:, :W]
    u = proj[:, W:]

    uext_ref[halo:halo + rows, :] = u
    uc = convw_ref[CONV_WIDTH - 1:CONV_WIDTH, :] * u + convb_ref[...]
    for j in range(1, CONV_WIDTH):
        start_row = halo - j * batch
        uc = uc + convw_ref[CONV_WIDTH - 1 - j:CONV_WIDTH - j, :] * uext_ref[start_row:start_row + rows, :]
    uext_ref[0:halo, :] = u[rows - halo:rows, :]

    ucb = uc.astype(BF16)
    gates = [_dot(ucb[:, n * LRU_BLOCK:(n + 1) * LRU_BLOCK], wgate_ref[n]) for n in range(LRU_BLOCKS)]
    ga = jnp.concatenate([g[:, :LRU_BLOCK] for g in gates], axis=1)
    gx = jnp.concatenate([g[:, LRU_BLOCK:] for g in gates], axis=1)
    tr = jnp.tanh(ga + ba_ref[...])
    ti = jnp.tanh(gx + bx_ref[...])
    half_c = (-0.5 * LRU_C) * jax.nn.log_sigmoid(lam_ref[...])
    neg_log_a = tr * half_c + half_c
    a = jnp.exp2(neg_log_a * -LOG2E)
    gain2 = jnp.tanh(neg_log_a) * (a * a + 1.0)
    gain = gain2 * lax.rsqrt(jnp.maximum(gain2, jnp.finfo(F32).tiny))
    b2 = gain * uc * (ti + 1.0)

    carry = hc_ref[...]
    hs = []
    for t in range(steps):
        carry = a[t * batch:(t + 1) * batch, :] * carry + b2[t * batch:(t + 1) * batch, :]
        hs.append(carry)
    hc_ref[...] = carry
    hseq2 = jnp.concatenate(hs, axis=0)

    x = gate_branch
    inner = x * (GELU_K1 * (x * x) + GELU_K0)
    y = (hseq2 * (0.25 * x) * (jnp.tanh(inner) + 1.0)).astype(BF16)
    out = h + _rms(_dot(y, wout_ref[...]), gpost_ref[...])
    out_buf[slot] = out.reshape(steps, batch, D_MODEL)
    start(tile_out(i, slot))

    @pl.when(i == n_tiles - 1)
    def _():
        if n_tiles >= 2:
            wait(tile_out(i - 1, 1 - slot))
        wait(tile_out(i, slot))


def _lru_layer(h, g_pre, g_post, w_in, conv_w, conv_b, w_gate_a, b_gate_a, w_gate_x, b_gate_x,
               lam, w_out):
    B, S, _ = h.shape
    W = LRU_WIDTH
    assert B % SUBLANES == 0, "a time step must fill whole sublane groups"
    steps = min(LRU_ROWS // B, S)
    n_tiles = S // steps
    halo = (CONV_WIDTH - 1) * B
    w_gate = (0.5 * jnp.concatenate([w_gate_a, w_gate_x], axis=-1)).astype(BF16)
    hbm = pl.BlockSpec(memory_space=pl.ANY)
    row = lambda v: v.astype(F32).reshape(1, W)
    return pl.pallas_call(
        functools.partial(_lru_kernel, steps=steps, batch=B, n_tiles=n_tiles),
        grid=(n_tiles,),
        in_specs=[hbm, _const_spec((1, D_MODEL)), _const_spec((1, D_MODEL)),
                  _const_spec((D_MODEL, 2 * W)), _const_spec((CONV_WIDTH, W)), _const_spec((1, W)),
                  _const_spec((LRU_BLOCKS, LRU_BLOCK, 2 * LRU_BLOCK)), _const_spec((1, W)),
                  _const_spec((1, W)), _const_spec((1, W)), _const_spec((W, D_MODEL))],
        out_specs=hbm,
        out_shape=jax.ShapeDtypeStruct((B, S, D_MODEL), F32),
        scratch_shapes=[pltpu.VMEM((2, steps, B, D_MODEL), F32), pltpu.VMEM((2, steps, B, D_MODEL), F32),
                        pltpu.SemaphoreType.DMA((2,)), pltpu.SemaphoreType.DMA((2,)),
                        pltpu.VMEM((steps * B + halo, W), F32), pltpu.VMEM((B, W), F32)],
        compiler_params=pltpu.CompilerParams(
            dimension_semantics=("arbitrary",), vmem_limit_bytes=VMEM_LIMIT_BYTES),
        name="rglru",
    )(h, g_pre.reshape(1, D_MODEL), g_post.reshape(1, D_MODEL),
      w_in.astype(BF16), conv_w.astype(F32), row(conv_b), w_gate, row(0.5 * b_gate_a),
      row(0.5 * b_gate_x), row(lam), w_out.astype(BF16))


def kernel(x, ml_w_in, ml_b_if, ml_head_norm, ml_w_out, lru_w_in, lru_conv_w, lru_conv_b, lru_w_gate_a, lru_b_gate_a, lru_w_gate_x, lru_b_gate_x, lru_lambda, lru_w_out, norm_pre_mix, norm_post_mix, norm_pre_ffn, norm_post_ffn, ffn_w_gate, ffn_w_up, ffn_w_down):
    B, S, D = x.shape
    depth = norm_pre_mix.shape[0]
    h = x
    for layer in range(depth):
        j = layer // 2
        if layer % 2 == 0:
            h = _mlstm_layer(h, j, norm_pre_mix[layer], norm_post_mix[layer], ml_w_in, ml_b_if[j],
                             ml_head_norm[j], ml_w_out)
        else:
            h = _lru_layer(h, norm_pre_mix[layer], norm_post_mix[layer], lru_w_in[j], lru_conv_w[j],
                           lru_conv_b[j], lru_w_gate_a[j], lru_b_gate_a[j], lru_w_gate_x[j],
                           lru_b_gate_x[j], lru_lambda[j], lru_w_out[j])
        h = _ffn_layer(h.reshape(B * S, D), layer, norm_pre_ffn[layer], norm_post_ffn[layer],
                       ffn_w_gate, ffn_w_up, ffn_w_down).reshape(B, S, D)
    return h
```

```python
import functools
import math

import jax
import jax.numpy as jnp
from jax import lax
from jax.experimental import pallas as pl
from jax.experimental.pallas import tpu as pltpu

D_MODEL = 1024
ML_HEADS = 8
ML_QK_DIM = 64
ML_V_DIM = 128
ML_QK = ML_HEADS * ML_QK_DIM
ML_V = ML_HEADS * ML_V_DIM
GATE_CAP = 15.0
ML_M_INIT = -1e30
LRU_WIDTH = D_MODEL
LRU_BLOCKS = 4
LRU_BLOCK = LRU_WIDTH // LRU_BLOCKS
CONV_WIDTH = 4
LRU_C = 8.0
D_FF = 2816
EPS = 1e-6
LOG2E = math.log2(math.e)
GELU_K0 = math.sqrt(2.0 / math.pi)
GELU_K1 = GELU_K0 * 0.044715

SUBLANES = 8
LANES = 128
VMEM_LIMIT_BYTES = 56 * 1024 * 1024

ML_CHUNK = 128
ML_TILE = 512
LRU_ROWS = 1024
FFN_TILE = 512
FFN_CHUNK = 256

F32 = jnp.float32
BF16 = jnp.bfloat16


def _rms(x, g):
    return x * lax.rsqrt(jnp.mean(x * x, axis=-1, keepdims=True) + EPS) * g


def _sigmoid(x):
    return 0.5 * jnp.tanh(0.5 * x) + 0.5


def _dot(a, b):
    return jnp.dot(a, b, preferred_element_type=F32)


def _dot_nt(a, b):
    return lax.dot_general(a, b, (((1,), (1,)), ((), ())), preferred_element_type=F32)


def _const_spec(shape):
    zeros = (0,) * len(shape)
    return pl.BlockSpec(shape, lambda *_: zeros, pipeline_mode=pl.Buffered(1))


def _layer_spec(stacked, layer):
    shape = stacked.shape[1:]
    zeros = (0,) * len(shape)
    return pl.BlockSpec((None,) + shape, lambda *_: (layer,) + zeros, pipeline_mode=pl.Buffered(1))


def _ffn_kernel(h_ref, gpre_ref, gpost_ref, wg_ref, wu_ref, wd_ref, o_ref, *, fc):
    h = h_ref[...]
    xn = _rms(h, gpre_ref[...])
    acc = None
    for c in range(D_FF // fc):
        cols = slice(c * fc, (c + 1) * fc)
        half_g = 0.5 * _dot(xn, wg_ref[:, cols])
        a = (jnp.tanh(half_g) * half_g + half_g) * _dot(xn, wu_ref[:, cols])
        part = _dot(a, wd_ref[cols, :])
        acc = part if acc is None else acc + part
    o_ref[...] = h + _rms(acc, gpost_ref[...])


def _ffn_layer(h, layer, g_pre, g_post, w_gate, w_up, w_down):
    n_tok = h.shape[0]
    tile = min(FFN_TILE, n_tok)
    tok_spec = pl.BlockSpec((tile, D_MODEL), lambda i: (i, 0))
    return pl.pallas_call(
        functools.partial(_ffn_kernel, fc=FFN_CHUNK),
        grid=(n_tok // tile,),
        in_specs=[tok_spec, _const_spec((1, D_MODEL)), _const_spec((1, D_MODEL)),
                  _layer_spec(w_gate, layer), _layer_spec(w_up, layer), _layer_spec(w_down, layer)],
        out_specs=tok_spec,
        out_shape=jax.ShapeDtypeStruct((n_tok, D_MODEL), F32),
        compiler_params=pltpu.CompilerParams(
            dimension_semantics=("parallel",), vmem_limit_bytes=VMEM_LIMIT_BYTES),
        name="swiglu",
    )(h, g_pre.reshape(1, D_MODEL), g_post.reshape(1, D_MODEL), w_gate, w_up, w_down)


def _soft_cap(z):
    return GATE_CAP * jnp.tanh(z / GATE_CAP)


def _lane_prefix(x, combine, identity):
    n = x.shape[-1]
    lane = lax.broadcasted_iota(jnp.int32, x.shape, x.ndim - 1)
    d = 1
    while d < n:
        x = combine(x, jnp.where(lane >= d, pltpu.roll(x, d, axis=x.ndim - 1), identity))
        d *= 2
    return x


def _mlstm_step(hx_ref, hy_ref, gpre_ref, gpost_ref, win_ref, wkg_ref, brow_ref, hnorm_ref, wout_ref,
                write, read, o_ref, c_ref, m_ref, hs_ref, *, tile, chunk):
    H, dk, dv, L = ML_HEADS, ML_QK_DIM, ML_V_DIM, chunk
    n_chunks = tile // L
    log2_q_scale = -0.5 * math.log2(dk)
    q_w, v_w, gate_w, kt_w, prt_w, rr2_w, ea_w, dec_w = write
    q_r, v_r, gate_r, kt_r, prt_r, rr2_r, ea_r, dec_r = read

    causal = (lax.broadcasted_iota(jnp.int32, (L, L), 1)
              <= lax.broadcasted_iota(jnp.int32, (L, L), 0))
    ones_aug = jnp.ones((L, dv), F32)
    zeros_kk = jnp.zeros((dk, dk), F32)

    def scores(c):
        r0 = c * L
        return [_dot(q_r[hh, r0:r0 + L, :], kt_r[hh * dk:(hh + 1) * dk, r0:r0 + L]) for hh in range(H)]

    def state_matmul(c, hh, s_mat):
        r0 = c * L
        q_h = q_r[hh, r0:r0 + L, :]
        kt_h = kt_r[hh * dk:(hh + 1) * dk, r0:r0 + L]
        v_h = v_r[r0:r0 + L, hh * dv:(hh + 1) * dv]
        cm2 = prt_r[c, :, hh:hh + 1]
        w_inter = prt_r[c, :, H + hh:H + hh + 1]
        c_prev = c_ref[hh]
        p_mat = jnp.where(causal, jnp.exp2(cm2 + rr2_r[hh:hh + 1, r0:r0 + L]) * s_mat, 0.0)
        lhs = jnp.concatenate(
            [jnp.concatenate([p_mat, q_h * w_inter], axis=1),
             jnp.concatenate([kt_h * ea_r[hh:hh + 1, r0:r0 + L], zeros_kk], axis=1)], axis=0)
        rhs = jnp.concatenate([jnp.concatenate([v_h, ones_aug], axis=1), c_prev], axis=0)
        r = _dot(lhs, rhs)
        c_ref[hh] = dec_r[c, hh:hh + 1, :] * c_prev + r[L:L + dk]
        return r

    def epilogue(c, hh, r):
        r0 = c * L
        emt = prt_r[c, :, 2 * H + hh:2 * H + hh + 1]
        den = jnp.maximum(jnp.abs(r[0:L, dv:2 * dv]), emt)
        h_out = r[0:L, 0:dv] / den
        hn = h_out * lax.rsqrt(jnp.mean(h_out * h_out, axis=-1, keepdims=True) + EPS)
        hs_ref[r0:r0 + L, hh * dv:(hh + 1) * dv] = hn * gate_r[r0:r0 + L, hh * dv:(hh + 1) * dv]

    PW = 2 * LANES
    gate_rows = []

    def kg_piece(t0):
        def run():
            kg = _dot_nt(wkg_ref[...], xn[t0:t0 + PW, :])
            kt_w[:, t0:t0 + PW] = kg[0:ML_QK, :]
            gate_rows.append(_soft_cap(kg[ML_QK:ML_QK + 2 * H, :] + brow_ref[...]))
        return run

    def q_piece(c0):
        def run():
            q4 = _dot(xn, win_ref[:, c0:c0 + PW])
            for j in range(PW // dk):
                q_w[c0 // dk + j] = q4[:, j * dk:(j + 1) * dk]
        return run

    def v_piece(c0):
        def run():
            v_w[:, c0:c0 + PW] = _dot(xn, win_ref[:, 2 * ML_QK + c0:2 * ML_QK + c0 + PW])
        return run

    def gate_piece(c0):
        def run():
            o = _dot(xn, win_ref[:, 2 * ML_QK + ML_V + c0:2 * ML_QK + ML_V + c0 + PW])
            gate_w[:, c0:c0 + PW] = _sigmoid(o) * hnorm_ref[:, c0:c0 + PW]
        return run

    def bookkeeping():
        grow = jnp.concatenate(gate_rows, axis=1)
        li_all = grow[0:H, :]
        lf_all = jax.nn.log_sigmoid(grow[H:2 * H, :])
        m_prev = m_ref[...]
        for c in range(n_chunks):
            li = li_all[:, c * L:(c + 1) * L]
            b = _lane_prefix(lf_all[:, c * L:(c + 1) * L], jnp.add, 0.0)
            rr = li - b
            rr_max = _lane_prefix(rr, jnp.maximum, -jnp.inf)
            g_tot = b[:, L - 1:L]
            m_t = b + jnp.maximum(rr_max, m_prev)
            m_new = jnp.maximum(g_tot + m_prev, g_tot + rr_max[:, L - 1:L])
            dec_w[c] = jnp.exp(g_tot + m_prev - m_new)
            ea_w[:, c * L:(c + 1) * L] = jnp.exp(g_tot + rr - m_new)
            rr2_w[:, c * L:(c + 1) * L] = rr * LOG2E
            per_row = jnp.concatenate(
                [(b - m_t) * LOG2E + log2_q_scale, jnp.exp(b + m_prev - m_t) * 2.0 ** log2_q_scale,
                 jnp.exp(-m_t), jnp.zeros((L - 3 * H, L), F32)], axis=0)
            prt_w[c] = per_row.T
            m_prev = m_new
        m_ref[...] = m_prev

    pieces = ([kg_piece(t0) for t0 in range(0, tile, PW)] + [q_piece(c0) for c0 in range(0, ML_QK, PW)]
              + [v_piece(c0) for c0 in range(0, ML_V, PW)] + [gate_piece(c0) for c0 in range(0, ML_V, PW)])
    slots = n_chunks * 3
    per_slot = -(-len(pieces) // slots)

    def issue_pieces():
        for _ in range(per_slot):
            if pieces:
                pieces.pop(0)()

    s_mat = scores(0)
    xn = _rms(hx_ref[...], gpre_ref[...])
    for c in range(n_chunks):
        res = {}
        for hh in range(H):
            res[hh] = state_matmul(c, hh, s_mat[hh])
            if hh in (1, 4, 7):
                issue_pieces()
                for done in (hh - 2, hh - 1, hh) if hh > 1 else (0, 1):
                    epilogue(c, done, res[done])
        if c + 1 < n_chunks:
            s_mat = scores(c + 1)
    while pieces:
        pieces.pop(0)()
    bookkeeping()

    y = _dot(hs_ref[...], wout_ref[...])
    o_ref[...] = hy_ref[...] + _rms(y, gpost_ref[...])


def _mlstm_kernel(hx_ref, hy_ref, gpre_ref, gpost_ref, win_ref, brow_ref, hnorm_ref,
                  wout_ref, o_ref, c_ref, m_ref, hs_ref, wkg_ref, *proj_refs, tile, chunk, tiles_per_seq):
    s = pl.program_id(0)
    n_buf = len(proj_refs) // 2
    bufs = (proj_refs[:n_buf], proj_refs[n_buf:])

    @pl.when(s == 0)
    def _():
        for ref in bufs[1]:
            ref[...] = jnp.zeros_like(ref)
        wkg_ref[0:ML_QK, :] = win_ref[:, ML_QK:2 * ML_QK].T
        gate_cols = jnp.concatenate(
            [win_ref[:, 2 * ML_QK + 2 * ML_V:], jnp.zeros((D_MODEL, LANES - 2 * ML_HEADS), F32)], axis=1)
        wkg_ref[ML_QK:ML_QK + 2 * ML_HEADS, :] = gate_cols.T[0:2 * ML_HEADS, :]

    @pl.when(s % tiles_per_seq == 0)
    def _():
        m_ref[...] = jnp.full_like(m_ref, ML_M_INIT)

    @pl.when(jnp.maximum(s - 1, 0) % tiles_per_seq == 0)
    def _():
        c_ref[...] = jnp.zeros_like(c_ref)

    step = functools.partial(
        _mlstm_step, hx_ref, hy_ref, gpre_ref, gpost_ref, win_ref, wkg_ref, brow_ref, hnorm_ref,
        wout_ref, o_ref=o_ref, c_ref=c_ref, m_ref=m_ref, hs_ref=hs_ref, tile=tile, chunk=chunk)
    pl.when(s % 2 == 0)(lambda: step(bufs[0], bufs[1]))
    pl.when(s % 2 == 1)(lambda: step(bufs[1], bufs[0]))


def _mlstm_layer(h, layer, g_pre, g_post, w_in, b_if, head_norm, w_out):
    B, S, _ = h.shape
    H = ML_HEADS
    tile = min(ML_TILE, S)
    chunk = min(ML_CHUNK, tile)
    tiles_per_seq = S // tile
    n_tiles = B * tiles_per_seq
    b_row = b_if.astype(F32).reshape(2 * H, 1)

    def ahead(s):
        k = jnp.minimum(s, n_tiles - 1)
        return (k // tiles_per_seq, k % tiles_per_seq, 0)

    def behind(s):
        k = jnp.maximum(s - 1, 0)
        return (k // tiles_per_seq, k % tiles_per_seq, 0)

    block = (None, tile, D_MODEL)
    n_chunks = tile // chunk
    stage_bufs = [pltpu.VMEM((H, tile, ML_QK_DIM), F32),
                  pltpu.VMEM((tile, ML_V), F32),
                  pltpu.VMEM((tile, ML_V), F32),
                  pltpu.VMEM((ML_QK, tile), F32),
                  pltpu.VMEM((n_chunks, chunk, LANES), F32),
                  pltpu.VMEM((H, tile), F32),
                  pltpu.VMEM((H, tile), F32),
                  pltpu.VMEM((n_chunks, H, 1), F32)]
    return pl.pallas_call(
        functools.partial(_mlstm_kernel, tile=tile, chunk=chunk, tiles_per_seq=tiles_per_seq),
        grid=(n_tiles + 1,),
        in_specs=[pl.BlockSpec(block, ahead), pl.BlockSpec(block, behind),
                  _const_spec((1, D_MODEL)), _const_spec((1, D_MODEL)),
                  _layer_spec(w_in, layer), _const_spec((2 * H, 1)),
                  _const_spec((1, ML_V)), _layer_spec(w_out, layer)],
        out_specs=pl.BlockSpec(block, behind),
        out_shape=jax.ShapeDtypeStruct((B, S, D_MODEL), F32),
        scratch_shapes=[pltpu.VMEM((H, ML_QK_DIM, 2 * ML_V_DIM), F32),
                        pltpu.VMEM((H, 1), F32),
                        pltpu.VMEM((tile, ML_V), F32),
                        pltpu.VMEM((ML_QK + 2 * H, D_MODEL), F32)] + stage_bufs + stage_bufs,
        compiler_params=pltpu.CompilerParams(
            dimension_semantics=("arbitrary",), vmem_limit_bytes=VMEM_LIMIT_BYTES),
        name="mlstm",
    )(h, h, g_pre.reshape(1, D_MODEL), g_post.reshape(1, D_MODEL), w_in,
      b_row, head_norm.reshape(1, ML_V), w_out)


def _lru_kernel(h_hbm, gpre_ref, gpost_ref, win_ref, convw_ref, convb_ref, wgate_ref, ba_ref,
                bx_ref, lam_ref, wout_ref, o_hbm, in_buf, out_buf, in_sem, out_sem, uext_ref, hc_ref,
                *, steps, batch, n_tiles):
    W = LRU_WIDTH
    rows = steps * batch
    halo = (CONV_WIDTH - 1) * batch
    i = pl.program_id(0)
    slot = lax.rem(i, 2)

    def tile_in(tile, to_slot):
        t0 = pl.multiple_of(tile * steps, steps)
        return [pltpu.make_async_copy(h_hbm.at[b, pl.ds(t0, steps), :], in_buf.at[to_slot, :, b, :],
                                      in_sem.at[to_slot]) for b in range(batch)]

    def tile_out(tile, from_slot):
        t0 = pl.multiple_of(tile * steps, steps)
        return [pltpu.make_async_copy(out_buf.at[from_slot, :, b, :], o_hbm.at[b, pl.ds(t0, steps), :],
                                      out_sem.at[from_slot]) for b in range(batch)]

    def start(copies):
        for c in copies:
            c.start()

    def wait(copies):
        for c in copies:
            c.wait()

    @pl.when(i == 0)
    def _():
        uext_ref[0:halo, :] = jnp.zeros((halo, W), F32)
        hc_ref[...] = jnp.zeros_like(hc_ref)
        start(tile_in(0, 0))

    @pl.when(i + 1 < n_tiles)
    def _():
        start(tile_in(i + 1, 1 - slot))

    wait(tile_in(i, slot))

    @pl.when(i >= 2)
    def _():
        wait(tile_out(i - 2, slot))

    h = in_buf[slot].reshape(rows, D_MODEL)
    xn = _rms(h, gpre_ref[...]).astype(BF16)
    proj = _dot(xn, win_ref[...])
    gate_branch = proj[:, :W]
    u = proj[:, W:]

    uext_ref[halo:halo + rows, :] = u
    uc = convw_ref[CONV_WIDTH - 1:CONV_WIDTH, :] * u + convb_ref[...]
    for j in range(1, CONV_WIDTH):
        start_row = halo - j * batch
        uc = uc + convw_ref[CONV_WIDTH - 1 - j:CONV_WIDTH - j, :] * uext_ref[start_row:start_row + rows, :]
    uext_ref[0:halo, :] = u[rows - halo:rows, :]

    ucb = uc.astype(BF16)
    gates = [_dot(ucb[:, n * LRU_BLOCK:(n + 1) * LRU_BLOCK], wgate_ref[n]) for n in range(LRU_BLOCKS)]
    ga = jnp.concatenate([g[:, :LRU_BLOCK] for g in gates], axis=1)
    gx = jnp.concatenate([g[:, LRU_BLOCK:] for g in gates], axis=1)
    tr = jnp.tanh(ga + ba_ref[...])
    ti = jnp.tanh(gx + bx_ref[...])
    half_c = (-0.5 * LRU_C) * jax.nn.log_sigmoid(lam_ref[...])
    neg_log_a = tr * half_c + half_c
    a = jnp.exp2(neg_log_a * -LOG2E)
    gain2 = jnp.tanh(neg_log_a) * (a * a + 1.0)
    gain = gain2 * lax.rsqrt(jnp.maximum(gain2, jnp.finfo(F32).tiny))
    b2 = gain * uc * (ti + 1.0)

    carry = hc_ref[...]
    hs = []
    for t in range(steps):
        carry = a[t * batch:(t + 1) * batch, :] * carry + b2[t * batch:(t + 1) * batch, :]
        hs.append(carry)
    hc_ref[...] = carry
    hseq2 = jnp.concatenate(hs, axis=0)

    x = gate_branch
    inner = x * (GELU_K1 * (x * x) + GELU_K0)
    y = (hseq2 * (0.25 * x) * (jnp.tanh(inner) + 1.0)).astype(BF16)
    out = h + _rms(_dot(y, wout_ref[...]), gpost_ref[...])
    out_buf[slot] = out.reshape(steps, batch, D_MODEL)
    start(tile_out(i, slot))

    @pl.when(i == n_tiles - 1)
    def _():
        if n_tiles >= 2:
            wait(tile_out(i - 1, 1 - slot))
        wait(tile_out(i, slot))


def _lru_layer(h, g_pre, g_post, w_in, conv_w, conv_b, w_gate_a, b_gate_a, w_gate_x, b_gate_x,
               lam, w_out):
    B, S, _ = h.shape
    W = LRU_WIDTH
    assert B % SUBLANES == 0, "a time step must fill whole sublane groups"
    steps = min(LRU_ROWS // B, S)
    n_tiles = S // steps
    halo = (CONV_WIDTH - 1) * B
    w_gate = (0.5 * jnp.concatenate([w_gate_a, w_gate_x], axis=-1)).astype(BF16)
    hbm = pl.BlockSpec(memory_space=pl.ANY)
    row = lambda v: v.astype(F32).reshape(1, W)
    return pl.pallas_call(
        functools.partial(_lru_kernel, steps=steps, batch=B, n_tiles=n_tiles),
        grid=(n_tiles,),
        in_specs=[hbm, _const_spec((1, D_MODEL)), _const_spec((1, D_MODEL)),
                  _const_spec((D_MODEL, 2 * W)), _const_spec((CONV_WIDTH, W)), _const_spec((1, W)),
                  _const_spec((LRU_BLOCKS, LRU_BLOCK, 2 * LRU_BLOCK)), _const_spec((1, W)),
                  _const_spec((1, W)), _const_spec((1, W)), _const_spec((W, D_MODEL))],
        out_specs=hbm,
        out_shape=jax.ShapeDtypeStruct((B, S, D_MODEL), F32),
        scratch_shapes=[pltpu.VMEM((2, steps, B, D_MODEL), F32), pltpu.VMEM((2, steps, B, D_MODEL), F32),
                        pltpu.SemaphoreType.DMA((2,)), pltpu.SemaphoreType.DMA((2,)),
                        pltpu.VMEM((steps * B + halo, W), F32), pltpu.VMEM((B, W), F32)],
        compiler_params=pltpu.CompilerParams(
            dimension_semantics=("arbitrary",), vmem_limit_bytes=VMEM_LIMIT_BYTES),
        name="rglru",
    )(h, g_pre.reshape(1, D_MODEL), g_post.reshape(1, D_MODEL),
      w_in.astype(BF16), conv_w.astype(F32), row(conv_b), w_gate, row(0.5 * b_gate_a),
      row(0.5 * b_gate_x), row(lam), w_out.astype(BF16))


def kernel(x, ml_w_in, ml_b_if, ml_head_norm, ml_w_out, lru_w_in, lru_conv_w, lru_conv_b, lru_w_gate_a, lru_b_gate_a, lru_w_gate_x, lru_b_gate_x, lru_lambda, lru_w_out, norm_pre_mix, norm_post_mix, norm_pre_ffn, norm_post_ffn, ffn_w_gate, ffn_w_up, ffn_w_down):
    B, S, D = x.shape
    depth = norm_pre_mix.shape[0]
    h = x
    for layer in range(depth):
        j = layer // 2
        if layer % 2 == 0:
            h = _mlstm_layer(h, j, norm_pre_mix[layer], norm_post_mix[layer], ml_w_in, ml_b_if[j],
                             ml_head_norm[j], ml_w_out)
        else:
            h = _lru_layer(h, norm_pre_mix[layer], norm_post_mix[layer], lru_w_in[j], lru_conv_w[j],
                           lru_conv_b[j], lru_w_gate_a[j], lru_b_gate_a[j], lru_w_gate_x[j],
                           lru_b_gate_x[j], lru_lambda[j], lru_w_out[j])
        h = _ffn_layer(h.reshape(B * S, D), layer, norm_pre_ffn[layer], norm_post_ffn[layer],
                       ffn_w_gate, ffn_w_up, ffn_w_down).reshape(B, S, D)
    return h
```

```python
import functools
import math

import jax
import jax.numpy as jnp
from jax import lax
from jax.experimental import pallas as pl
from jax.experimental.pallas import tpu as pltpu

D_MODEL = 1024
ML_HEADS = 8
ML_QK_DIM = 64
ML_V_DIM = 128
ML_QK = ML_HEADS * ML_QK_DIM
ML_V = ML_HEADS * ML_V_DIM
GATE_CAP = 15.0
ML_M_INIT = -1e30
LRU_WIDTH = D_MODEL
LRU_BLOCKS = 4
LRU_BLOCK = LRU_WIDTH // LRU_BLOCKS
CONV_WIDTH = 4
LRU_C = 8.0
D_FF = 2816
EPS = 1e-6
LOG2E = math.log2(math.e)
GELU_K0 = math.sqrt(2.0 / math.pi)
GELU_K1 = GELU_K0 * 0.044715

SUBLANES = 8
LANES = 128
V7X_VMEM_BYTES = 64 * 1024 * 1024
VMEM_LIMIT_BYTES = V7X_VMEM_BYTES * 7 // 8

ML_CHUNK = 128
ML_TILE = 512
LRU_ROWS = 1024
FFN_TILE = 512
FFN_CHUNK = 256

F32 = jnp.float32
BF16 = jnp.bfloat16


def _rms(x, g):
    return x * lax.rsqrt(jnp.mean(x * x, axis=-1, keepdims=True) + EPS) * g


def _sigmoid(x):
    return 0.5 * jnp.tanh(0.5 * x) + 0.5


def _dot(a, b):
    return jnp.dot(a, b, preferred_element_type=F32)


def _dot_nt(a, b):
    return lax.dot_general(a, b, (((1,), (1,)), ((), ())), preferred_element_type=F32)


def _const_spec(shape):
    zeros = (0,) * len(shape)
    return pl.BlockSpec(shape, lambda *_: zeros, pipeline_mode=pl.Buffered(1))


def _layer_spec(stacked, layer):
    shape = stacked.shape[1:]
    zeros = (0,) * len(shape)
    return pl.BlockSpec((None,) + shape, lambda *_: (layer,) + zeros, pipeline_mode=pl.Buffered(1))


def _ffn_kernel(h_ref, gpre_ref, gpost_ref, wg_ref, wu_ref, wd_ref, o_ref, *, fc):
    h = h_ref[...]
    xn = _rms(h, gpre_ref[...])
    acc = None
    for c in range(D_FF // fc):
        cols = slice(c * fc, (c + 1) * fc)
        half_g = 0.5 * _dot(xn, wg_ref[:, cols])
        a = (jnp.tanh(half_g) * half_g + half_g) * _dot(xn, wu_ref[:, cols])
        part = _dot(a, wd_ref[cols, :])
        acc = part if acc is None else acc + part
    o_ref[...] = h + _rms(acc, gpost_ref[...])


def _ffn_layer(h, layer, g_pre, g_post, w_gate, w_up, w_down):
    n_tok = h.shape[0]
    tile = min(FFN_TILE, n_tok)
    tok_spec = pl.BlockSpec((tile, D_MODEL), lambda i: (i, 0))
    return pl.pallas_call(
        functools.partial(_ffn_kernel, fc=FFN_CHUNK),
        grid=(n_tok // tile,),
        in_specs=[tok_spec, _const_spec((1, D_MODEL)), _const_spec((1, D_MODEL)),
                  _layer_spec(w_gate, layer), _layer_spec(w_up, layer), _layer_spec(w_down, layer)],
        out_specs=tok_spec,
        out_shape=jax.ShapeDtypeStruct((n_tok, D_MODEL), F32),
        compiler_params=pltpu.CompilerParams(
            dimension_semantics=("parallel",), vmem_limit_bytes=VMEM_LIMIT_BYTES),
        name="swiglu",
    )(h, g_pre.reshape(1, D_MODEL), g_post.reshape(1, D_MODEL), w_gate, w_up, w_down)


def _soft_cap(z):
    return GATE_CAP * jnp.tanh(z / GATE_CAP)


def _lane_prefix(x, combine, identity):
    n = x.shape[-1]
    lane = lax.broadcasted_iota(jnp.int32, x.shape, x.ndim - 1)
    d = 1
    while d < n:
        x = combine(x, jnp.where(lane >= d, pltpu.roll(x, d, axis=x.ndim - 1), identity))
        d *= 2
    return x


def _mlstm_step(hx_ref, hy_ref, gpre_ref, gpost_ref, win_ref, wkg_ref, brow_ref, hnorm_ref, wout_ref,
                write, read, o_ref, c_ref, m_ref, hs_ref, *, tile, chunk):
    H, dk, dv, L = ML_HEADS, ML_QK_DIM, ML_V_DIM, chunk
    n_chunks = tile // L
    log2_q_scale = -0.5 * math.log2(dk)
    q_w, v_w, gate_w, kt_w, prt_w, rr2_w, ea_w, dec_w = write
    q_r, v_r, gate_r, kt_r, prt_r, rr2_r, ea_r, dec_r = read

    causal = (lax.broadcasted_iota(jnp.int32, (L, L), 1)
              <= lax.broadcasted_iota(jnp.int32, (L, L), 0))
    ones_aug = jnp.ones((L, dv), F32)
    zeros_kk = jnp.zeros((dk, dk), F32)

    def scores(c):
        r0 = c * L
        return [_dot(q_r[hh, r0:r0 + L, :], kt_r[hh * dk:(hh + 1) * dk, r0:r0 + L]) for hh in range(H)]

    def state_matmul(c, hh, s_mat):
        r0 = c * L
        q_h = q_r[hh, r0:r0 + L, :]
        kt_h = kt_r[hh * dk:(hh + 1) * dk, r0:r0 + L]
        v_h = v_r[r0:r0 + L, hh * dv:(hh + 1) * dv]
        cm2 = prt_r[c, :, hh:hh + 1]
        w_inter = prt_r[c, :, H + hh:H + hh + 1]
        c_prev = c_ref[hh]
        p_mat = jnp.where(causal, jnp.exp2(cm2 + rr2_r[hh:hh + 1, r0:r0 + L]) * s_mat, 0.0)
        lhs = jnp.concatenate(
            [jnp.concatenate([p_mat, q_h * w_inter], axis=1),
             jnp.concatenate([kt_h * ea_r[hh:hh + 1, r0:r0 + L], zeros_kk], axis=1)], axis=0)
        rhs = jnp.concatenate([jnp.concatenate([v_h, ones_aug], axis=1), c_prev], axis=0)
        r = _dot(lhs, rhs)
        c_ref[hh] = dec_r[c, hh:hh + 1, :] * c_prev + r[L:L + dk]
        return r

    def epilogue(c, hh, r):
        r0 = c * L
        emt = prt_r[c, :, 2 * H + hh:2 * H + hh + 1]
        den = jnp.maximum(jnp.abs(r[0:L, dv:2 * dv]), emt)
        h_out = r[0:L, 0:dv] / den
        hn = h_out * lax.rsqrt(jnp.mean(h_out * h_out, axis=-1, keepdims=True) + EPS)
        hs_ref[r0:r0 + L, hh * dv:(hh + 1) * dv] = hn * gate_r[r0:r0 + L, hh * dv:(hh + 1) * dv]

    PW = 2 * LANES
    gate_rows = []

    def kg_piece(t0):
        def run():
            kg = _dot_nt(wkg_ref[...], xn[t0:t0 + PW, :])
            kt_w[:, t0:t0 + PW] = kg[0:ML_QK, :]
            gate_rows.append(_soft_cap(kg[ML_QK:ML_QK + 2 * H, :] + brow_ref[...]))
        return run

    def q_piece(c0):
        def run():
            q4 = _dot(xn, win_ref[:, c0:c0 + PW])
            for j in range(PW // dk):
                q_w[c0 // dk + j] = q4[:, j * dk:(j + 1) * dk]
        return run

    def v_piece(c0):
        def run():
            v_w[:, c0:c0 + PW] = _dot(xn, win_ref[:, 2 * ML_QK + c0:2 * ML_QK + c0 + PW])
        return run

    def gate_piece(c0):
        def run():
            o = _dot(xn, win_ref[:, 2 * ML_QK + ML_V + c0:2 * ML_QK + ML_V + c0 + PW])
            gate_w[:, c0:c0 + PW] = _sigmoid(o) * hnorm_ref[:, c0:c0 + PW]
        return run

    def bookkeeping():
        grow = jnp.concatenate(gate_rows, axis=1)
        li_all = grow[0:H, :]
        lf_all = jax.nn.log_sigmoid(grow[H:2 * H, :])
        m_prev = m_ref[...]
        for c in range(n_chunks):
            li = li_all[:, c * L:(c + 1) * L]
            b = _lane_prefix(lf_all[:, c * L:(c + 1) * L], jnp.add, 0.0)
            rr = li - b
            rr_max = _lane_prefix(rr, jnp.maximum, -jnp.inf)
            g_tot = b[:, L - 1:L]
            m_t = b + jnp.maximum(rr_max, m_prev)
            m_new = jnp.maximum(g_tot + m_prev, g_tot + rr_max[:, L - 1:L])
            dec_w[c] = jnp.exp(g_tot + m_prev - m_new)
            ea_w[:, c * L:(c + 1) * L] = jnp.exp(g_tot + rr - m_new)
            rr2_w[:, c * L:(c + 1) * L] = rr * LOG2E
            per_row = jnp.concatenate(
                [(b - m_t) * LOG2E + log2_q_scale, jnp.exp(b + m_prev - m_t) * 2.0 ** log2_q_scale,
                 jnp.exp(-m_t), jnp.zeros((L - 3 * H, L), F32)], axis=0)
            prt_w[c] = per_row.T
            m_prev = m_new
        m_ref[...] = m_prev

    pieces = ([kg_piece(t0) for t0 in range(0, tile, PW)] + [q_piece(c0) for c0 in range(0, ML_QK, PW)]
              + [v_piece(c0) for c0 in range(0, ML_V, PW)] + [gate_piece(c0) for c0 in range(0, ML_V, PW)])
    slots = n_chunks * 3
    per_slot = -(-len(pieces) // slots)

    def issue_pieces():
        for _ in range(per_slot):
            if pieces:
                pieces.pop(0)()

    s_mat = scores(0)
    xn = _rms(hx_ref[...], gpre_ref[...])
    for c in range(n_chunks):
        res = {}
        for hh in range(H):
            res[hh] = state_matmul(c, hh, s_mat[hh])
            if hh in (1, 4, 7):
                issue_pieces()
                for done in (hh - 2, hh - 1, hh) if hh > 1 else (0, 1):
                    epilogue(c, done, res[done])
        if c + 1 < n_chunks:
            s_mat = scores(c + 1)
    while pieces:
        pieces.pop(0)()
    bookkeeping()

    y = _dot(hs_ref[...], wout_ref[...])
    o_ref[...] = hy_ref[...] + _rms(y, gpost_ref[...])


def _mlstm_kernel(hx_ref, hy_ref, gpre_ref, gpost_ref, win_ref, brow_ref, hnorm_ref,
                  wout_ref, o_ref, c_ref, m_ref, hs_ref, wkg_ref, *proj_refs, tile, chunk, tiles_per_seq):
    s = pl.program_id(0)
    n_buf = len(proj_refs) // 2
    bufs = (proj_refs[:n_buf], proj_refs[n_buf:])

    @pl.when(s == 0)
    def _():
        for ref in bufs[1]:
            ref[...] = jnp.zeros_like(ref)
        wkg_ref[0:ML_QK, :] = win_ref[:, ML_QK:2 * ML_QK].T
        gate_cols = jnp.concatenate(
            [win_ref[:, 2 * ML_QK + 2 * ML_V:], jnp.zeros((D_MODEL, LANES - 2 * ML_HEADS), F32)], axis=1)
        wkg_ref[ML_QK:ML_QK + 2 * ML_HEADS, :] = gate_cols.T[0:2 * ML_HEADS, :]

    @pl.when(s % tiles_per_seq == 0)
    def _():
        m_ref[...] = jnp.full_like(m_ref, ML_M_INIT)

    @pl.when(jnp.maximum(s - 1, 0) % tiles_per_seq == 0)
    def _():
        c_ref[...] = jnp.zeros_like(c_ref)

    step = functools.partial(
        _mlstm_step, hx_ref, hy_ref, gpre_ref, gpost_ref, win_ref, wkg_ref, brow_ref, hnorm_ref,
        wout_ref, o_ref=o_ref, c_ref=c_ref, m_ref=m_ref, hs_ref=hs_ref, tile=tile, chunk=chunk)
    pl.when(s % 2 == 0)(lambda: step(bufs[0], bufs[1]))
    pl.when(s % 2 == 1)(lambda: step(bufs[1], bufs[0]))


def _mlstm_layer(h, layer, g_pre, g_post, w_in, b_if, head_norm, w_out):
    B, S, _ = h.shape
    H = ML_HEADS
    tile = min(ML_TILE, S)
    chunk = min(ML_CHUNK, tile)
    tiles_per_seq = S // tile
    n_tiles = B * tiles_per_seq
    b_row = b_if.astype(F32).reshape(2 * H, 1)

    def ahead(s):
        k = jnp.minimum(s, n_tiles - 1)
        return (k // tiles_per_seq, k % tiles_per_seq, 0)

    def behind(s):
        k = jnp.maximum(s - 1, 0)
        return (k // tiles_per_seq, k % tiles_per_seq, 0)

    block = (None, tile, D_MODEL)
    n_chunks = tile // chunk
    stage_bufs = [pltpu.VMEM((H, tile, ML_QK_DIM), F32),
                  pltpu.VMEM((tile, ML_V), F32),
                  pltpu.VMEM((tile, ML_V), F32),
                  pltpu.VMEM((ML_QK, tile), F32),
                  pltpu.VMEM((n_chunks, chunk, LANES), F32),
                  pltpu.VMEM((H, tile), F32),
                  pltpu.VMEM((H, tile), F32),
                  pltpu.VMEM((n_chunks, H, 1), F32)]
    return pl.pallas_call(
        functools.partial(_mlstm_kernel, tile=tile, chunk=chunk, tiles_per_seq=tiles_per_seq),
        grid=(n_tiles + 1,),
        in_specs=[pl.BlockSpec(block, ahead), pl.BlockSpec(block, behind),
                  _const_spec((1, D_MODEL)), _const_spec((1, D_MODEL)),
                  _layer_spec(w_in, layer), _const_spec((2 * H, 1)),
                  _const_spec((1, ML_V)), _layer_spec(w_out, layer)],
        out_specs=pl.BlockSpec(block, behind),
        out_shape=jax.ShapeDtypeStruct((B, S, D_MODEL), F32),
        scratch_shapes=[pltpu.VMEM((H, ML_QK_DIM, 2 * ML_V_DIM), F32),
                        pltpu.VMEM((H, 1), F32),
                        pltpu.VMEM((tile, ML_V), F32),
                        pltpu.VMEM((ML_QK + 2 * H, D_MODEL), F32)] + stage_bufs + stage_bufs,
        compiler_params=pltpu.CompilerParams(
            dimension_semantics=("arbitrary",), vmem_limit_bytes=VMEM_LIMIT_BYTES),
        name="mlstm",
    )(h, h, g_pre.reshape(1, D_MODEL), g_post.reshape(1, D_MODEL), w_in,
      b_row, head_norm.reshape(1, ML_V), w_out)


def _lru_kernel(h_hbm, gpre_ref, gpost_ref, win_ref, convw_ref, convb_ref, wgate_ref, ba_ref,
                bx_ref, lam_ref, wout_ref, o_hbm, in_buf, out_buf, in_sem, out_sem, uext_ref, hc_ref,
                *, steps, batch, n_tiles):
    W = LRU_WIDTH
    rows = steps * batch
    halo = (CONV_WIDTH - 1) * batch
    i = pl.program_id(0)
    slot = lax.rem(i, 2)

    def tile_in(tile, to_slot):
        t0 = pl.multiple_of(tile * steps, steps)
        return [pltpu.make_async_copy(h_hbm.at[b, pl.ds(t0, steps), :], in_buf.at[to_slot, :, b, :],
                                      in_sem.at[to_slot]) for b in range(batch)]

    def tile_out(tile, from_slot):
        t0 = pl.multiple_of(tile * steps, steps)
        return [pltpu.make_async_copy(out_buf.at[from_slot, :, b, :], o_hbm.at[b, pl.ds(t0, steps), :],
                                      out_sem.at[from_slot]) for b in range(batch)]

    def start(copies):
        for c in copies:
            c.start()

    def wait(copies):
        for c in copies:
            c.wait()

    @pl.when(i == 0)
    def _():
        uext_ref[0:halo, :] = jnp.zeros((halo, W), F32)
        hc_ref[...] = jnp.zeros_like(hc_ref)
        start(tile_in(0, 0))

    @pl.when(i + 1 < n_tiles)
    def _():
        start(tile_in(i + 1, 1 - slot))

    wait(tile_in(i, slot))

    @pl.when(i >= 2)
    def _():
        wait(tile_out(i - 2, slot))

    h = in_buf[slot].reshape(rows, D_MODEL)
    xn = _rms(h, gpre_ref[...]).astype(BF16)
    proj = _dot(xn, win_ref[...])
    gate_branch = proj[:, :W]
    u = proj[:, W:]

    uext_ref[halo:halo + rows, :] = u
    uc = convw_ref[CONV_WIDTH - 1:CONV_WIDTH, :] * u + convb_ref[...]
    for j in range(1, CONV_WIDTH):
        start_row = halo - j * batch
        uc = uc + convw_ref[CONV_WIDTH - 1 - j:CONV_WIDTH - j, :] * uext_ref[start_row:start_row + rows, :]
    uext_ref[0:halo, :] = u[rows - halo:rows, :]

    ucb = uc.astype(BF16)
    gates = [_dot(ucb[:, n * LRU_BLOCK:(n + 1) * LRU_BLOCK], wgate_ref[n]) for n in range(LRU_BLOCKS)]
    ga = jnp.concatenate([g[:, :LRU_BLOCK] for g in gates], axis=1)
    gx = jnp.concatenate([g[:, LRU_BLOCK:] for g in gates], axis=1)
    tr = jnp.tanh(ga + ba_ref[...])
    ti = jnp.tanh(gx + bx_ref[...])
    half_c = (-0.5 * LRU_C) * jax.nn.log_sigmoid(lam_ref[...])
    neg_log_a = tr * half_c + half_c
    a = jnp.exp2(neg_log_a * -LOG2E)
    gain2 = jnp.tanh(neg_log_a) * (a * a + 1.0)
    gain = gain2 * lax.rsqrt(jnp.maximum(gain2, jnp.finfo(F32).tiny))
    b2 = gain * uc * (ti + 1.0)

    carry = hc_ref[...]
    hs = []
    for t in range(steps):
        carry = a[t * batch:(t + 1) * batch, :] * carry + b2[t * batch:(t + 1) * batch, :]
        hs.append(carry)
    hc_ref[...] = carry
    hseq2 = jnp.concatenate(hs, axis=0)

    x = gate_branch
    inner = x * (GELU_K1 * (x * x) + GELU_K0)
    y = (hseq2 * (0.25 * x) * (jnp.tanh(inner) + 1.0)).astype(BF16)
    out = h + _rms(_dot(y, wout_ref[...]), gpost_ref[...])
    out_buf[slot] = out.reshape(steps, batch, D_MODEL)
    start(tile_out(i, slot))

    @pl.when(i == n_tiles - 1)
    def _():
        if n_tiles >= 2:
            wait(tile_out(i - 1, 1 - slot))
        wait(tile_out(i, slot))


def _lru_layer(h, g_pre, g_post, w_in, conv_w, conv_b, w_gate_a, b_gate_a, w_gate_x, b_gate_x,
               lam, w_out):
    B, S, _ = h.shape
    W = LRU_WIDTH
    assert B % SUBLANES == 0, "a time step must fill whole sublane groups"
    steps = min(LRU_ROWS // B, S)
    n_tiles = S // steps
    halo = (CONV_WIDTH - 1) * B
    w_gate = (0.5 * jnp.concatenate([w_gate_a, w_gate_x], axis=-1)).astype(BF16)
    hbm = pl.BlockSpec(memory_space=pl.ANY)
    row = lambda v: v.astype(F32).reshape(1, W)
    return pl.pallas_call(
        functools.partial(_lru_kernel, steps=steps, batch=B, n_tiles=n_tiles),
        grid=(n_tiles,),
        in_specs=[hbm, _const_spec((1, D_MODEL)), _const_spec((1, D_MODEL)),
                  _const_spec((D_MODEL, 2 * W)), _const_spec((CONV_WIDTH, W)), _const_spec((1, W)),
                  _const_spec((LRU_BLOCKS, LRU_BLOCK, 2 * LRU_BLOCK)), _const_spec((1, W)),
                  _const_spec((1, W)), _const_spec((1, W)), _const_spec((W, D_MODEL))],
        out_specs=hbm,
        out_shape=jax.ShapeDtypeStruct((B, S, D_MODEL), F32),
        scratch_shapes=[pltpu.VMEM((2, steps, B, D_MODEL), F32), pltpu.VMEM((2, steps, B, D_MODEL), F32),
                        pltpu.SemaphoreType.DMA((2,)), pltpu.SemaphoreType.DMA((2,)),
                        pltpu.VMEM((steps * B + halo, W), F32), pltpu.VMEM((B, W), F32)],
        compiler_params=pltpu.CompilerParams(
            dimension_semantics=("arbitrary",), vmem_limit_bytes=VMEM_LIMIT_BYTES),
        name="rglru",
    )(h, g_pre.reshape(1, D_MODEL), g_post.reshape(1, D_MODEL),
      w_in.astype(BF16), conv_w.astype(F32), row(conv_b), w_gate, row(0.5 * b_gate_a),
      row(0.5 * b_gate_x), row(lam), w_out.astype(BF16))


def kernel(x, ml_w_in, ml_b_if, ml_head_norm, ml_w_out, lru_w_in, lru_conv_w, lru_conv_b, lru_w_gate_a, lru_b_gate_a, lru_w_gate_x, lru_b_gate_x, lru_lambda, lru_w_out, norm_pre_mix, norm_post_mix, norm_pre_ffn, norm_post_ffn, ffn_w_gate, ffn_w_up, ffn_w_down):
    B, S, D = x.shape
    depth = norm_pre_mix.shape[0]
    h = x
    for layer in range(depth):
        j = layer // 2
        if layer % 2 == 0:
            h = _mlstm_layer(h, j, norm_pre_mix[layer], norm_post_mix[layer], ml_w_in, ml_b_if[j],
                             ml_head_norm[j], ml_w_out)
        else:
            h = _lru_layer(h, norm_pre_mix[layer], norm_post_mix[layer], lru_w_in[j], lru_conv_w[j],
                           lru_conv_b[j], lru_w_gate_a[j], lru_b_gate_a[j], lru_w_gate_x[j],
                           lru_b_gate_x[j], lru_lambda[j], lru_w_out[j])
        h = _ffn_layer(h.reshape(B * S, D), layer, norm_pre_ffn[layer], norm_post_ffn[layer],
                       ffn_w_gate, ffn_w_up, ffn_w_down).reshape(B, S, D)
    return h
```

```python
import functools
import math

import jax
import jax.numpy as jnp
from jax import lax
from jax.experimental import pallas as pl
from jax.experimental.pallas import tpu as pltpu

D_MODEL = 1024
ML_HEADS = 8
ML_QK_DIM = 64
ML_V_DIM = 128
ML_QK = ML_HEADS * ML_QK_DIM
ML_V = ML_HEADS * ML_V_DIM
GATE_CAP = 15.0
ML_M_INIT = -1e30
LRU_WIDTH = D_MODEL
LRU_BLOCKS = 4
LRU_BLOCK = LRU_WIDTH // LRU_BLOCKS
CONV_WIDTH = 4
LRU_C = 8.0
D_FF = 2816
EPS = 1e-6
LOG2E = math.log2(math.e)
GELU_K0 = math.sqrt(2.0 / math.pi)
GELU_K1 = GELU_K0 * 0.044715

SUBLANES = 8
LANES = 128
V7X_VMEM_BYTES = 64 * 1024 * 1024
VMEM_LIMIT_BYTES = V7X_VMEM_BYTES * 7 // 8

ML_CHUNK = 128
ML_TILE = 512
LRU_ROWS = 1024
FFN_TILE = 512
FFN_CHUNK = 256

F32 = jnp.float32
BF16 = jnp.bfloat16


def _rms(x, g):
    return x * lax.rsqrt(jnp.mean(x * x, axis=-1, keepdims=True) + EPS) * g


def _sigmoid(x):
    return 0.5 * jnp.tanh(0.5 * x) + 0.5


def _dot(a, b):
    return jnp.dot(a, b, preferred_element_type=F32)


def _dot_nt(a, b):
    return lax.dot_general(a, b, (((1,), (1,)), ((), ())), preferred_element_type=F32)


def _const_spec(shape):
    zeros = (0,) * len(shape)
    return pl.BlockSpec(shape, lambda *_: zeros, pipeline_mode=pl.Buffered(1))


def _layer_spec(stacked, layer):
    shape = stacked.shape[1:]
    zeros = (0,) * len(shape)
    return pl.BlockSpec((None,) + shape, lambda *_: (layer,) + zeros, pipeline_mode=pl.Buffered(1))


def _ffn_kernel(h_ref, gpre_ref, gpost_ref, wg_ref, wu_ref, wd_ref, o_ref, *, fc):
    h = h_ref[...]
    xn = _rms(h, gpre_ref[...])
    acc = None
    for c in range(D_FF // fc):
        cols = slice(c * fc, (c + 1) * fc)
        half_g = 0.5 * _dot(xn, wg_ref[:, cols])
        a = (jnp.tanh(half_g) * half_g + half_g) * _dot(xn, wu_ref[:, cols])
        part = _dot(a, wd_ref[cols, :])
        acc = part if acc is None else acc + part
    o_ref[...] = h + _rms(acc, gpost_ref[...])


def _ffn_layer(h, layer, g_pre, g_post, w_gate, w_up, w_down):
    n_tok = h.shape[0]
    tile = min(FFN_TILE, n_tok)
    tok_spec = pl.BlockSpec((tile, D_MODEL), lambda i: (i, 0))
    return pl.pallas_call(
        functools.partial(_ffn_kernel, fc=FFN_CHUNK),
        grid=(n_tok // tile,),
        in_specs=[tok_spec, _const_spec((1, D_MODEL)), _const_spec((1, D_MODEL)),
                  _layer_spec(w_gate, layer), _layer_spec(w_up, layer), _layer_spec(w_down, layer)],
        out_specs=tok_spec,
        out_shape=jax.ShapeDtypeStruct((n_tok, D_MODEL), F32),
        compiler_params=pltpu.CompilerParams(
            dimension_semantics=("parallel",), vmem_limit_bytes=VMEM_LIMIT_BYTES),
        name="swiglu",
    )(h, g_pre.reshape(1, D_MODEL), g_post.reshape(1, D_MODEL), w_gate, w_up, w_down)


def _soft_cap(z):
    return GATE_CAP * jnp.tanh(z / GATE_CAP)


def _lane_prefix(x, combine, identity):
    n = x.shape[-1]
    lane = lax.broadcasted_iota(jnp.int32, x.shape, x.ndim - 1)
    d = 1
    while d < n:
        x = combine(x, jnp.where(lane >= d, pltpu.roll(x, d, axis=x.ndim - 1), identity))
        d *= 2
    return x


def _mlstm_step(hx_ref, hy_ref, gpre_ref, gpost_ref, win_ref, wkg_ref, brow_ref, hnorm_ref, wout_ref,
                write, read, o_ref, c_ref, m_ref, hs_ref, *, tile, chunk):
    H, dk, dv, L = ML_HEADS, ML_QK_DIM, ML_V_DIM, chunk
    n_chunks = tile // L
    log2_q_scale = -0.5 * math.log2(dk)
    q_w, v_w, gate_w, kt_w, prt_w, rr2_w, ea_w, dec_w = write
    q_r, v_r, gate_r, kt_r, prt_r, rr2_r, ea_r, dec_r = read

    causal = (lax.broadcasted_iota(jnp.int32, (L, L), 1)
              <= lax.broadcasted_iota(jnp.int32, (L, L), 0))
    ones_aug = jnp.ones((L, dv), F32)
    zeros_kk = jnp.zeros((dk, dk), F32)

    def scores(c):
        r0 = c * L
        return [_dot(q_r[hh, r0:r0 + L, :], kt_r[hh * dk:(hh + 1) * dk, r0:r0 + L]) for hh in range(H)]

    def state_matmul(c, hh, s_mat):
        r0 = c * L
        q_h = q_r[hh, r0:r0 + L, :]
        kt_h = kt_r[hh * dk:(hh + 1) * dk, r0:r0 + L]
        v_h = v_r[r0:r0 + L, hh * dv:(hh + 1) * dv]
        cm2 = prt_r[c, :, hh:hh + 1]
        w_inter = prt_r[c, :, H + hh:H + hh + 1]
        c_prev = c_ref[hh]
        p_mat = jnp.where(causal, jnp.exp2(cm2 + rr2_r[hh:hh + 1, r0:r0 + L]) * s_mat, 0.0)
        lhs = jnp.concatenate(
            [jnp.concatenate([p_mat, q_h * w_inter], axis=1),
             jnp.concatenate([kt_h * ea_r[hh:hh + 1, r0:r0 + L], zeros_kk], axis=1)], axis=0)
        rhs = jnp.concatenate([jnp.concatenate([v_h, ones_aug], axis=1), c_prev], axis=0)
        r = _dot(lhs, rhs)
        c_ref[hh] = dec_r[c, hh:hh + 1, :] * c_prev + r[L:L + dk]
        return r

    def epilogue(c, hh, r):
        r0 = c * L
        emt = prt_r[c, :, 2 * H + hh:2 * H + hh + 1]
        den = jnp.maximum(jnp.abs(r[0:L, dv:2 * dv]), emt)
        h_out = r[0:L, 0:dv] / den
        hn = h_out * lax.rsqrt(jnp.mean(h_out * h_out, axis=-1, keepdims=True) + EPS)
        hs_ref[r0:r0 + L, hh * dv:(hh + 1) * dv] = hn * gate_r[r0:r0 + L, hh * dv:(hh + 1) * dv]

    PW = 2 * LANES
    gate_rows = []

    def kg_piece(t0):
        def run():
            kg = _dot_nt(wkg_ref[...], xn[t0:t0 + PW, :])
            kt_w[:, t0:t0 + PW] = kg[0:ML_QK, :]
            gate_rows.append(_soft_cap(kg[ML_QK:ML_QK + 2 * H, :] + brow_ref[...]))
        return run

    def q_piece(c0):
        def run():
            q4 = _dot(xn, win_ref[:, c0:c0 + PW])
            for j in range(PW // dk):
                q_w[c0 // dk + j] = q4[:, j * dk:(j + 1) * dk]
        return run

    def v_piece(c0):
        def run():
            v_w[:, c0:c0 + PW] = _dot(xn, win_ref[:, 2 * ML_QK + c0:2 * ML_QK + c0 + PW])
        return run

    def gate_piece(c0):
        def run():
            o = _dot(xn, win_ref[:, 2 * ML_QK + ML_V + c0:2 * ML_QK + ML_V + c0 + PW])
            gate_w[:, c0:c0 + PW] = _sigmoid(o) * hnorm_ref[:, c0:c0 + PW]
        return run

    def bookkeeping():
        grow = jnp.concatenate(gate_rows, axis=1)
        li_all = grow[0:H, :]
        lf_all = jax.nn.log_sigmoid(grow[H:2 * H, :])
        m_prev = m_ref[...]
        for c in range(n_chunks):
            li = li_all[:, c * L:(c + 1) * L]
            b = _lane_prefix(lf_all[:, c * L:(c + 1) * L], jnp.add, 0.0)
            rr = li - b
            rr_max = _lane_prefix(rr, jnp.maximum, -jnp.inf)
            g_tot = b[:, L - 1:L]
            m_t = b + jnp.maximum(rr_max, m_prev)
            m_new = jnp.maximum(g_tot + m_prev, g_tot + rr_max[:, L - 1:L])
            dec_w[c] = jnp.exp(g_tot + m_prev - m_new)
            ea_w[:, c * L:(c + 1) * L] = jnp.exp(g_tot + rr - m_new)
            rr2_w[:, c * L:(c + 1) * L] = rr * LOG2E
            per_row = jnp.concatenate(
                [(b - m_t) * LOG2E + log2_q_scale, jnp.exp(b + m_prev - m_t) * 2.0 ** log2_q_scale,
                 jnp.exp(-m_t), jnp.zeros((L - 3 * H, L), F32)], axis=0)
            prt_w[c] = per_row.T
            m_prev = m_new
        m_ref[...] = m_prev

    pieces = ([kg_piece(t0) for t0 in range(0, tile, PW)] + [q_piece(c0) for c0 in range(0, ML_QK, PW)]
              + [v_piece(c0) for c0 in range(0, ML_V, PW)] + [gate_piece(c0) for c0 in range(0, ML_V, PW)])
    slots = n_chunks * 3
    per_slot = -(-len(pieces) // slots)

    def issue_pieces():
        for _ in range(per_slot):
            if pieces:
                pieces.pop(0)()

    s_mat = scores(0)
    xn = _rms(hx_ref[...], gpre_ref[...])
    for c in range(n_chunks):
        res = {}
        for hh in range(H):
            res[hh] = state_matmul(c, hh, s_mat[hh])
            if hh in (1, 4, 7):
                issue_pieces()
                for done in (hh - 2, hh - 1, hh) if hh > 1 else (0, 1):
                    epilogue(c, done, res[done])
        if c + 1 < n_chunks:
            s_mat = scores(c + 1)
    while pieces:
        pieces.pop(0)()
    bookkeeping()

    y = _dot(hs_ref[...], wout_ref[...])
    o_ref[...] = hy_ref[...] + _rms(y, gpost_ref[...])


def _mlstm_kernel(hx_ref, hy_ref, gpre_ref, gpost_ref, win_ref, brow_ref, hnorm_ref,
                  wout_ref, o_ref, c_ref, m_ref, hs_ref, wkg_ref, *proj_refs, tile, chunk, tiles_per_seq):
    s = pl.program_id(0)
    n_buf = len(proj_refs) // 2
    bufs = (proj_refs[:n_buf], proj_refs[n_buf:])

    @pl.when(s == 0)
    def _():
        for ref in bufs[1]:
            ref[...] = jnp.zeros_like(ref)
        wkg_ref[0:ML_QK, :] = win_ref[:, ML_QK:2 * ML_QK].T
        gate_cols = jnp.concatenate(
            [win_ref[:, 2 * ML_QK + 2 * ML_V:], jnp.zeros((D_MODEL, LANES - 2 * ML_HEADS), F32)], axis=1)
        wkg_ref[ML_QK:ML_QK + 2 * ML_HEADS, :] = gate_cols.T[0:2 * ML_HEADS, :]

    @pl.when(s % tiles_per_seq == 0)
    def _():
        m_ref[...] = jnp.full_like(m_ref, ML_M_INIT)

    @pl.when(jnp.maximum(s - 1, 0) % tiles_per_seq == 0)
    def _():
        c_ref[...] = jnp.zeros_like(c_ref)

    step = functools.partial(
        _mlstm_step, hx_ref, hy_ref, gpre_ref, gpost_ref, win_ref, wkg_ref, brow_ref, hnorm_ref,
        wout_ref, o_ref=o_ref, c_ref=c_ref, m_ref=m_ref, hs_ref=hs_ref, tile=tile, chunk=chunk)
    pl.when(s % 2 == 0)(lambda: step(bufs[0], bufs[1]))
    pl.when(s % 2 == 1)(lambda: step(bufs[1], bufs[0]))


def _mlstm_layer(h, layer, g_pre, g_post, w_in, b_if, head_norm, w_out):
    B, S, _ = h.shape
    H = ML_HEADS
    tile = min(ML_TILE, S)
    chunk = min(ML_CHUNK, tile)
    tiles_per_seq = S // tile
    n_tiles = B * tiles_per_seq
    b_row = b_if.astype(F32).reshape(2 * H, 1)

    def ahead(s):
        k = jnp.minimum(s, n_tiles - 1)
        return (k // tiles_per_seq, k % tiles_per_seq, 0)

    def behind(s):
        k = jnp.maximum(s - 1, 0)
        return (k // tiles_per_seq, k % tiles_per_seq, 0)

    block = (None, tile, D_MODEL)
    n_chunks = tile // chunk
    stage_bufs = [pltpu.VMEM((H, tile, ML_QK_DIM), F32),
                  pltpu.VMEM((tile, ML_V), F32),
                  pltpu.VMEM((tile, ML_V), F32),
                  pltpu.VMEM((ML_QK, tile), F32),
                  pltpu.VMEM((n_chunks, chunk, LANES), F32),
                  pltpu.VMEM((H, tile), F32),
                  pltpu.VMEM((H, tile), F32),
                  pltpu.VMEM((n_chunks, H, 1), F32)]
    return pl.pallas_call(
        functools.partial(_mlstm_kernel, tile=tile, chunk=chunk, tiles_per_seq=tiles_per_seq),
        grid=(n_tiles + 1,),
        in_specs=[pl.BlockSpec(block, ahead), pl.BlockSpec(block, behind),
                  _const_spec((1, D_MODEL)), _const_spec((1, D_MODEL)),
                  _layer_spec(w_in, layer), _const_spec((2 * H, 1)),
                  _const_spec((1, ML_V)), _layer_spec(w_out, layer)],
        out_specs=pl.BlockSpec(block, behind),
        out_shape=jax.ShapeDtypeStruct((B, S, D_MODEL), F32),
        scratch_shapes=[pltpu.VMEM((H, ML_QK_DIM, 2 * ML_V_DIM), F32),
                        pltpu.VMEM((H, 1), F32),
                        pltpu.VMEM((tile, ML_V), F32),
                        pltpu.VMEM((ML_QK + 2 * H, D_MODEL), F32)] + stage_bufs + stage_bufs,
        compiler_params=pltpu.CompilerParams(
            dimension_semantics=("arbitrary",), vmem_limit_bytes=VMEM_LIMIT_BYTES),
        name="mlstm",
    )(h, h, g_pre.reshape(1, D_MODEL), g_post.reshape(1, D_MODEL), w_in,
      b_row, head_norm.reshape(1, ML_V), w_out)


def _lru_kernel(h_hbm, gpre_ref, gpost_ref, win_ref, convw_ref, convb_ref, wgate_ref, ba_ref,
                bx_ref, lam_ref, wout_ref, o_hbm, in_buf, out_buf, in_sem, out_sem, uext_ref, hc_ref,
                *, steps, batch, n_tiles):
    W = LRU_WIDTH
    rows = steps * batch
    halo = (CONV_WIDTH - 1) * batch
    i = pl.program_id(0)
    slot = lax.rem(i, 2)

    def tile_in(tile, to_slot):
        t0 = pl.multiple_of(tile * steps, steps)
        half = steps // 2
        return [pltpu.make_async_copy(h_hbm.at[b, pl.ds(t0 + k * half, half), :],
                                      in_buf.at[to_slot, pl.ds(k * half, half), b, :],
                                      in_sem.at[to_slot]) for b in range(batch) for k in range(2)]

    def tile_out(tile, from_slot):
        t0 = pl.multiple_of(tile * steps, steps)
        half = steps // 2
        return [pltpu.make_async_copy(out_buf.at[from_slot, pl.ds(k * half, half), b, :],
                                      o_hbm.at[b, pl.ds(t0 + k * half, half), :],
                                      out_sem.at[from_slot]) for b in range(batch) for k in range(2)]

    def start(copies):
        for c in copies:
            c.start()

    def wait(copies):
        for c in copies:
            c.wait()

    @pl.when(i == 0)
    def _():
        uext_ref[0:halo, :] = jnp.zeros((halo, W), F32)
        hc_ref[...] = jnp.zeros_like(hc_ref)
        start(tile_in(0, 0))

    @pl.when(i + 1 < n_tiles)
    def _():
        start(tile_in(i + 1, 1 - slot))

    wait(tile_in(i, slot))

    @pl.when(i >= 2)
    def _():
        wait(tile_out(i - 2, slot))

    h = in_buf[slot].reshape(rows, D_MODEL)
    xn = _rms(h, gpre_ref[...]).astype(BF16)
    proj = _dot(xn, win_ref[...])
    gate_branch = proj[:, :W]
    u = proj[:, W:]

    uext_ref[halo:halo + rows, :] = u
    uc = convw_ref[CONV_WIDTH - 1:CONV_WIDTH, :] * u + convb_ref[...]
    for j in range(1, CONV_WIDTH):
        start_row = halo - j * batch
        uc = uc + convw_ref[CONV_WIDTH - 1 - j:CONV_WIDTH - j, :] * uext_ref[start_row:start_row + rows, :]
    uext_ref[0:halo, :] = u[rows - halo:rows, :]

    ucb = uc.astype(BF16)
    gates = [_dot(ucb[:, n * LRU_BLOCK:(n + 1) * LRU_BLOCK], wgate_ref[n]) for n in range(LRU_BLOCKS)]
    ga = jnp.concatenate([g[:, :LRU_BLOCK] for g in gates], axis=1)
    gx = jnp.concatenate([g[:, LRU_BLOCK:] for g in gates], axis=1)
    tr = jnp.tanh(ga + ba_ref[...])
    ti = jnp.tanh(gx + bx_ref[...])
    half_c = (-0.5 * LRU_C) * jax.nn.log_sigmoid(lam_ref[...])
    neg_log_a = tr * half_c + half_c
    a = jnp.exp2(neg_log_a * -LOG2E)
    gain2 = jnp.tanh(neg_log_a) * (a * a + 1.0)
    gain = gain2 * lax.rsqrt(jnp.maximum(gain2, jnp.finfo(F32).tiny))
    b2 = gain * uc * (ti + 1.0)

    carry = hc_ref[...]
    hs = []
    for t in range(steps):
        carry = a[t * batch:(t + 1) * batch, :] * carry + b2[t * batch:(t + 1) * batch, :]
        hs.append(carry)
    hc_ref[...] = carry
    hseq2 = jnp.concatenate(hs, axis=0)

    x = gate_branch
    inner = x * (GELU_K1 * (x * x) + GELU_K0)
    y = (hseq2 * (0.25 * x) * (jnp.tanh(inner) + 1.0)).astype(BF16)
    out = h + _rms(_dot(y, wout_ref[...]), gpost_ref[...])
    out_buf[slot] = out.reshape(steps, batch, D_MODEL)
    start(tile_out(i, slot))

    @pl.when(i == n_tiles - 1)
    def _():
        if n_tiles >= 2:
            wait(tile_out(i - 1, 1 - slot))
        wait(tile_out(i, slot))


def _lru_layer(h, g_pre, g_post, w_in, conv_w, conv_b, w_gate_a, b_gate_a, w_gate_x, b_gate_x,
               lam, w_out):
    B, S, _ = h.shape
    W = LRU_WIDTH
    assert B % SUBLANES == 0, "a time step must fill whole sublane groups"
    steps = min(LRU_ROWS // B, S)
    n_tiles = S // steps
    halo = (CONV_WIDTH - 1) * B
    w_gate = (0.5 * jnp.concatenate([w_gate_a, w_gate_x], axis=-1)).astype(BF16)
    hbm = pl.BlockSpec(memory_space=pl.ANY)
    row = lambda v: v.astype(F32).reshape(1, W)
    return pl.pallas_call(
        functools.partial(_lru_kernel, steps=steps, batch=B, n_tiles=n_tiles),
        grid=(n_tiles,),
        in_specs=[hbm, _const_spec((1, D_MODEL)), _const_spec((1, D_MODEL)),
                  _const_spec((D_MODEL, 2 * W)), _const_spec((CONV_WIDTH, W)), _const_spec((1, W)),
                  _const_spec((LRU_BLOCKS, LRU_BLOCK, 2 * LRU_BLOCK)), _const_spec((1, W)),
                  _const_spec((1, W)), _const_spec((1, W)), _const_spec((W, D_MODEL))],
        out_specs=hbm,
        out_shape=jax.ShapeDtypeStruct((B, S, D_MODEL), F32),
        scratch_shapes=[pltpu.VMEM((2, steps, B, D_MODEL), F32), pltpu.VMEM((2, steps, B, D_MODEL), F32),
                        pltpu.SemaphoreType.DMA((2,)), pltpu.SemaphoreType.DMA((2,)),
                        pltpu.VMEM((steps * B + halo, W), F32), pltpu.VMEM((B, W), F32)],
        compiler_params=pltpu.CompilerParams(
            dimension_semantics=("arbitrary",), vmem_limit_bytes=VMEM_LIMIT_BYTES),
        name="rglru",
    )(h, g_pre.reshape(1, D_MODEL), g_post.reshape(1, D_MODEL),
      w_in.astype(BF16), conv_w.astype(F32), row(conv_b), w_gate, row(0.5 * b_gate_a),
      row(0.5 * b_gate_x), row(lam), w_out.astype(BF16))


def kernel(x, ml_w_in, ml_b_if, ml_head_norm, ml_w_out, lru_w_in, lru_conv_w, lru_conv_b, lru_w_gate_a, lru_b_gate_a, lru_w_gate_x, lru_b_gate_x, lru_lambda, lru_w_out, norm_pre_mix, norm_post_mix, norm_pre_ffn, norm_post_ffn, ffn_w_gate, ffn_w_up, ffn_w_down):
    B, S, D = x.shape
    depth = norm_pre_mix.shape[0]
    h = x
    for layer in range(depth):
        j = layer // 2
        if layer % 2 == 0:
            h = _mlstm_layer(h, j, norm_pre_mix[layer], norm_post_mix[layer], ml_w_in, ml_b_if[j],
                             ml_head_norm[j], ml_w_out)
        else:
            h = _lru_layer(h, norm_pre_mix[layer], norm_post_mix[layer], lru_w_in[j], lru_conv_w[j],
                           lru_conv_b[j], lru_w_gate_a[j], lru_b_gate_a[j], lru_w_gate_x[j],
                           lru_b_gate_x[j], lru_lambda[j], lru_w_out[j])
        h = _ffn_layer(h.reshape(B * S, D), layer, norm_pre_ffn[layer], norm_post_ffn[layer],
                       ffn_w_gate, ffn_w_up, ffn_w_down).reshape(B, S, D)
    return h
```

```python
import functools
import math

import jax
import jax.numpy as jnp
from jax import lax
from jax.experimental import pallas as pl
from jax.experimental.pallas import tpu as pltpu

D_MODEL = 1024
ML_HEADS = 8
ML_QK_DIM = 64
ML_V_DIM = 128
ML_QK = ML_HEADS * ML_QK_DIM
ML_V = ML_HEADS * ML_V_DIM
GATE_CAP = 15.0
ML_M_INIT = -1e30
LRU_WIDTH = D_MODEL
LRU_BLOCKS = 4
LRU_BLOCK = LRU_WIDTH // LRU_BLOCKS
CONV_WIDTH = 4
LRU_C = 8.0
D_FF = 2816
EPS = 1e-6
LOG2E = math.log2(math.e)
GELU_K0 = math.sqrt(2.0 / math.pi)
GELU_K1 = GELU_K0 * 0.044715

SUBLANES = 8
LANES = 128
V7X_VMEM_BYTES = 64 * 1024 * 1024
VMEM_LIMIT_BYTES = V7X_VMEM_BYTES * 7 // 8

ML_CHUNK = 128
ML_TILE = 512
LRU_ROWS = 1024
FFN_TILE = 1024
FFN_LOAD_CHUNKS = 16
FFN_CHUNK = 256

F32 = jnp.float32
BF16 = jnp.bfloat16


def _rms(x, g):
    return x * lax.rsqrt(jnp.mean(x * x, axis=-1, keepdims=True) + EPS) * g


def _sigmoid(x):
    return 0.5 * jnp.tanh(0.5 * x) + 0.5


def _dot(a, b):
    return jnp.dot(a, b, preferred_element_type=F32)


def _dot_nt(a, b):
    return lax.dot_general(a, b, (((1,), (1,)), ((), ())), preferred_element_type=F32)


def _const_spec(shape):
    zeros = (0,) * len(shape)
    return pl.BlockSpec(shape, lambda *_: zeros, pipeline_mode=pl.Buffered(1))


def _layer_spec(stacked, layer):
    shape = stacked.shape[1:]
    zeros = (0,) * len(shape)
    return pl.BlockSpec((None,) + shape, lambda *_: (layer,) + zeros, pipeline_mode=pl.Buffered(1))


def _load_bf16(src_hbm, layer, dst_ref, stage_ref, sem, n_chunks):
    rows = dst_ref.shape[0] // n_chunks

    def copy(k):
        slot = k % 2
        return pltpu.make_async_copy(src_hbm.at[layer, pl.ds(k * rows, rows), :],
                                     stage_ref.at[slot], sem.at[slot])

    copy(0).start()
    for k in range(n_chunks):
        if k + 1 < n_chunks:
            copy(k + 1).start()
        copy(k).wait()
        dst_ref[k * rows:(k + 1) * rows, :] = stage_ref[k % 2].astype(BF16)


def _ffn_kernel(h_ref, gpre_ref, gpost_ref, wg_hbm, wu_hbm, wd_hbm, o_ref, wg_ref, wu_ref, wd_ref,
                stage_in_ref, stage_out_ref, sem, *, layer, fc):
    @pl.when(pl.program_id(0) == 0)
    def _():
        _load_bf16(wg_hbm, layer, wg_ref, stage_in_ref, sem, FFN_LOAD_CHUNKS)
        _load_bf16(wu_hbm, layer, wu_ref, stage_in_ref, sem, FFN_LOAD_CHUNKS)
        _load_bf16(wd_hbm, layer, wd_ref, stage_out_ref, sem, FFN_LOAD_CHUNKS)

    h = h_ref[...]
    xn = _rms(h, gpre_ref[...]).astype(BF16)
    acc = None
    for c in range(D_FF // fc):
        cols = slice(c * fc, (c + 1) * fc)
        half_g = 0.5 * _dot(xn, wg_ref[:, cols])
        a = ((jnp.tanh(half_g) * half_g + half_g) * _dot(xn, wu_ref[:, cols])).astype(BF16)
        part = _dot(a, wd_ref[cols, :])
        acc = part if acc is None else acc + part
    o_ref[...] = h + _rms(acc, gpost_ref[...])


def _ffn_layer(h, layer, g_pre, g_post, w_gate, w_up, w_down):
    n_tok = h.shape[0]
    tile = min(FFN_TILE, n_tok)
    tok_spec = pl.BlockSpec((tile, D_MODEL), lambda i: (i, 0))
    hbm = pl.BlockSpec(memory_space=pl.ANY)
    return pl.pallas_call(
        functools.partial(_ffn_kernel, layer=layer, fc=FFN_CHUNK),
        grid=(n_tok // tile,),
        in_specs=[tok_spec, _const_spec((1, D_MODEL)), _const_spec((1, D_MODEL)), hbm, hbm, hbm],
        out_specs=tok_spec,
        out_shape=jax.ShapeDtypeStruct((n_tok, D_MODEL), F32),
        scratch_shapes=[pltpu.VMEM((D_MODEL, D_FF), BF16), pltpu.VMEM((D_MODEL, D_FF), BF16),
                        pltpu.VMEM((D_FF, D_MODEL), BF16),
                        pltpu.VMEM((2, D_MODEL // FFN_LOAD_CHUNKS, D_FF), F32),
                        pltpu.VMEM((2, D_FF // FFN_LOAD_CHUNKS, D_MODEL), F32),
                        pltpu.SemaphoreType.DMA((2,))],
        compiler_params=pltpu.CompilerParams(
            dimension_semantics=("arbitrary",), vmem_limit_bytes=VMEM_LIMIT_BYTES),
        name="swiglu",
    )(h, g_pre.reshape(1, D_MODEL), g_post.reshape(1, D_MODEL), w_gate, w_up, w_down)


def _soft_cap(z):
    return GATE_CAP * jnp.tanh(z / GATE_CAP)


def _lane_prefix(x, combine, identity):
    n = x.shape[-1]
    lane = lax.broadcasted_iota(jnp.int32, x.shape, x.ndim - 1)
    d = 1
    while d < n:
        x = combine(x, jnp.where(lane >= d, pltpu.roll(x, d, axis=x.ndim - 1), identity))
        d *= 2
    return x


def _mlstm_step(hx_ref, hy_ref, gpre_ref, gpost_ref, win_ref, wkg_ref, brow_ref, hnorm_ref, wout_ref,
                write, read, o_ref, c_ref, m_ref, hs_ref, *, tile, chunk):
    H, dk, dv, L = ML_HEADS, ML_QK_DIM, ML_V_DIM, chunk
    n_chunks = tile // L
    log2_q_scale = -0.5 * math.log2(dk)
    q_w, v_w, gate_w, kt_w, prt_w, rr2_w, ea_w, dec_w = write
    q_r, v_r, gate_r, kt_r, prt_r, rr2_r, ea_r, dec_r = read

    causal = (lax.broadcasted_iota(jnp.int32, (L, L), 1)
              <= lax.broadcasted_iota(jnp.int32, (L, L), 0))
    ones_aug = jnp.ones((L, dv), F32)
    zeros_kk = jnp.zeros((dk, dk), F32)

    def scores(c):
        r0 = c * L
        return [_dot(q_r[hh, r0:r0 + L, :], kt_r[hh * dk:(hh + 1) * dk, r0:r0 + L]) for hh in range(H)]

    def state_matmul(c, hh, s_mat):
        r0 = c * L
        q_h = q_r[hh, r0:r0 + L, :]
        kt_h = kt_r[hh * dk:(hh + 1) * dk, r0:r0 + L]
        v_h = v_r[r0:r0 + L, hh * dv:(hh + 1) * dv]
        cm2 = prt_r[c, :, hh:hh + 1]
        w_inter = prt_r[c, :, H + hh:H + hh + 1]
        c_prev = c_ref[hh]
        p_mat = jnp.where(causal, jnp.exp2(cm2 + rr2_r[hh:hh + 1, r0:r0 + L]) * s_mat, 0.0)
        lhs = jnp.concatenate(
            [jnp.concatenate([p_mat, q_h * w_inter], axis=1),
             jnp.concatenate([kt_h * ea_r[hh:hh + 1, r0:r0 + L], zeros_kk], axis=1)], axis=0)
        rhs = jnp.concatenate([jnp.concatenate([v_h, ones_aug], axis=1), c_prev], axis=0)
        r = _dot(lhs, rhs)
        c_ref[hh] = dec_r[c, hh:hh + 1, :] * c_prev + r[L:L + dk]
        return r

    def epilogue(c, hh, r):
        r0 = c * L
        emt = prt_r[c, :, 2 * H + hh:2 * H + hh + 1]
        den = jnp.maximum(jnp.abs(r[0:L, dv:2 * dv]), emt)
        h_out = r[0:L, 0:dv] / den
        hn = h_out * lax.rsqrt(jnp.mean(h_out * h_out, axis=-1, keepdims=True) + EPS)
        hs_ref[r0:r0 + L, hh * dv:(hh + 1) * dv] = hn * gate_r[r0:r0 + L, hh * dv:(hh + 1) * dv]

    PW = 2 * LANES
    gate_rows = []

    def kg_piece(t0):
        def run():
            kg = _dot_nt(wkg_ref[...], xn[t0:t0 + PW, :])
            kt_w[:, t0:t0 + PW] = kg[0:ML_QK, :]
            gate_rows.append(_soft_cap(kg[ML_QK:ML_QK + 2 * H, :] + brow_ref[...]))
        return run

    def q_piece(c0):
        def run():
            q4 = _dot(xn, win_ref[:, c0:c0 + PW])
            for j in range(PW // dk):
                q_w[c0 // dk + j] = q4[:, j * dk:(j + 1) * dk]
        return run

    def v_piece(c0):
        def run():
            v_w[:, c0:c0 + PW] = _dot(xn, win_ref[:, 2 * ML_QK + c0:2 * ML_QK + c0 + PW])
        return run

    def gate_piece(c0):
        def run():
            o = _dot(xn, win_ref[:, 2 * ML_QK + ML_V + c0:2 * ML_QK + ML_V + c0 + PW])
            gate_w[:, c0:c0 + PW] = _sigmoid(o) * hnorm_ref[:, c0:c0 + PW]
        return run

    def bookkeeping():
        grow = jnp.concatenate(gate_rows, axis=1)
        li_all = grow[0:H, :]
        lf_all = jax.nn.log_sigmoid(grow[H:2 * H, :])
        m_prev = m_ref[...]
        for c in range(n_chunks):
            li = li_all[:, c * L:(c + 1) * L]
            b = _lane_prefix(lf_all[:, c * L:(c + 1) * L], jnp.add, 0.0)
            rr = li - b
            rr_max = _lane_prefix(rr, jnp.maximum, -jnp.inf)
            g_tot = b[:, L - 1:L]
            m_t = b + jnp.maximum(rr_max, m_prev)
            m_new = jnp.maximum(g_tot + m_prev, g_tot + rr_max[:, L - 1:L])
            dec_w[c] = jnp.exp(g_tot + m_prev - m_new)
            ea_w[:, c * L:(c + 1) * L] = jnp.exp(g_tot + rr - m_new)
            rr2_w[:, c * L:(c + 1) * L] = rr * LOG2E
            per_row = jnp.concatenate(
                [(b - m_t) * LOG2E + log2_q_scale, jnp.exp(b + m_prev - m_t) * 2.0 ** log2_q_scale,
                 jnp.exp(-m_t), jnp.zeros((L - 3 * H, L), F32)], axis=0)
            prt_w[c] = per_row.T
            m_prev = m_new
        m_ref[...] = m_prev

    pieces = ([kg_piece(t0) for t0 in range(0, tile, PW)] + [q_piece(c0) for c0 in range(0, ML_QK, PW)]
              + [v_piece(c0) for c0 in range(0, ML_V, PW)] + [gate_piece(c0) for c0 in range(0, ML_V, PW)])
    slots = n_chunks * 3
    per_slot = -(-len(pieces) // slots)

    def issue_pieces():
        for _ in range(per_slot):
            if pieces:
                pieces.pop(0)()

    s_mat = scores(0)
    xn = _rms(hx_ref[...], gpre_ref[...])
    for c in range(n_chunks):
        res = {}
        for hh in range(H):
            res[hh] = state_matmul(c, hh, s_mat[hh])
            if hh in (1, 4, 7):
                issue_pieces()
                for done in (hh - 2, hh - 1, hh) if hh > 1 else (0, 1):
                    epilogue(c, done, res[done])
        if c + 1 < n_chunks:
            s_mat = scores(c + 1)
    while pieces:
        pieces.pop(0)()
    bookkeeping()

    y = _dot(hs_ref[...], wout_ref[...])
    o_ref[...] = hy_ref[...] + _rms(y, gpost_ref[...])


def _mlstm_kernel(hx_ref, hy_ref, gpre_ref, gpost_ref, win_ref, brow_ref, hnorm_ref,
                  wout_ref, o_ref, c_ref, m_ref, hs_ref, wkg_ref, *proj_refs, tile, chunk, tiles_per_seq):
    s = pl.program_id(0)
    n_buf = len(proj_refs) // 2
    bufs = (proj_refs[:n_buf], proj_refs[n_buf:])

    @pl.when(s == 0)
    def _():
        for ref in bufs[1]:
            ref[...] = jnp.zeros_like(ref)
        wkg_ref[0:ML_QK, :] = win_ref[:, ML_QK:2 * ML_QK].T
        gate_cols = jnp.concatenate(
            [win_ref[:, 2 * ML_QK + 2 * ML_V:], jnp.zeros((D_MODEL, LANES - 2 * ML_HEADS), F32)], axis=1)
        wkg_ref[ML_QK:ML_QK + 2 * ML_HEADS, :] = gate_cols.T[0:2 * ML_HEADS, :]

    @pl.when(s % tiles_per_seq == 0)
    def _():
        m_ref[...] = jnp.full_like(m_ref, ML_M_INIT)

    @pl.when(jnp.maximum(s - 1, 0) % tiles_per_seq == 0)
    def _():
        c_ref[...] = jnp.zeros_like(c_ref)

    step = functools.partial(
        _mlstm_step, hx_ref, hy_ref, gpre_ref, gpost_ref, win_ref, wkg_ref, brow_ref, hnorm_ref,
        wout_ref, o_ref=o_ref, c_ref=c_ref, m_ref=m_ref, hs_ref=hs_ref, tile=tile, chunk=chunk)
    pl.when(s % 2 == 0)(lambda: step(bufs[0], bufs[1]))
    pl.when(s % 2 == 1)(lambda: step(bufs[1], bufs[0]))


def _mlstm_layer(h, layer, g_pre, g_post, w_in, b_if, head_norm, w_out):
    B, S, _ = h.shape
    H = ML_HEADS
    tile = min(ML_TILE, S)
    chunk = min(ML_CHUNK, tile)
    tiles_per_seq = S // tile
    n_tiles = B * tiles_per_seq
    b_row = b_if.astype(F32).reshape(2 * H, 1)

    def ahead(s):
        k = jnp.minimum(s, n_tiles - 1)
        return (k // tiles_per_seq, k % tiles_per_seq, 0)

    def behind(s):
        k = jnp.maximum(s - 1, 0)
        return (k // tiles_per_seq, k % tiles_per_seq, 0)

    block = (None, tile, D_MODEL)
    n_chunks = tile // chunk
    stage_bufs = [pltpu.VMEM((H, tile, ML_QK_DIM), F32),
                  pltpu.VMEM((tile, ML_V), F32),
                  pltpu.VMEM((tile, ML_V), F32),
                  pltpu.VMEM((ML_QK, tile), F32),
                  pltpu.VMEM((n_chunks, chunk, LANES), F32),
                  pltpu.VMEM((H, tile), F32),
                  pltpu.VMEM((H, tile), F32),
                  pltpu.VMEM((n_chunks, H, 1), F32)]
    return pl.pallas_call(
        functools.partial(_mlstm_kernel, tile=tile, chunk=chunk, tiles_per_seq=tiles_per_seq),
        grid=(n_tiles + 1,),
        in_specs=[pl.BlockSpec(block, ahead), pl.BlockSpec(block, behind),
                  _const_spec((1, D_MODEL)), _const_spec((1, D_MODEL)),
                  _layer_spec(w_in, layer), _const_spec((2 * H, 1)),
                  _const_spec((1, ML_V)), _layer_spec(w_out, layer)],
        out_specs=pl.BlockSpec(block, behind),
        out_shape=jax.ShapeDtypeStruct((B, S, D_MODEL), F32),
        scratch_shapes=[pltpu.VMEM((H, ML_QK_DIM, 2 * ML_V_DIM), F32),
                        pltpu.VMEM((H, 1), F32),
                        pltpu.VMEM((tile, ML_V), F32),
                        pltpu.VMEM((ML_QK + 2 * H, D_MODEL), F32)] + stage_bufs + stage_bufs,
        compiler_params=pltpu.CompilerParams(
            dimension_semantics=("arbitrary",), vmem_limit_bytes=VMEM_LIMIT_BYTES),
        name="mlstm",
    )(h, h, g_pre.reshape(1, D_MODEL), g_post.reshape(1, D_MODEL), w_in,
      b_row, head_norm.reshape(1, ML_V), w_out)


def _lru_kernel(h_hbm, gpre_ref, gpost_ref, win_ref, convw_ref, convb_ref, wgate_ref, ba_ref,
                bx_ref, lam_ref, wout_ref, o_hbm, in_buf, out_buf, in_sem, out_sem, uext_ref, hc_ref,
                *, steps, batch, n_tiles):
    W = LRU_WIDTH
    rows = steps * batch
    halo = (CONV_WIDTH - 1) * batch
    i = pl.program_id(0)
    slot = lax.rem(i, 2)

    def tile_in(tile, to_slot):
        t0 = pl.multiple_of(tile * steps, steps)
        return [pltpu.make_async_copy(h_hbm.at[b, pl.ds(t0, steps), :], in_buf.at[to_slot, :, b, :],
                                      in_sem.at[to_slot]) for b in range(batch)]

    def tile_out(tile, from_slot):
        t0 = pl.multiple_of(tile * steps, steps)
        return [pltpu.make_async_copy(out_buf.at[from_slot, :, b, :], o_hbm.at[b, pl.ds(t0, steps), :],
                                      out_sem.at[from_slot]) for b in range(batch)]

    def start(copies):
        for c in copies:
            c.start()

    def wait(copies):
        for c in copies:
            c.wait()

    @pl.when(i == 0)
    def _():
        uext_ref[0:halo, :] = jnp.zeros((halo, W), F32)
        hc_ref[...] = jnp.zeros_like(hc_ref)
        start(tile_in(0, 0))

    @pl.when(i + 1 < n_tiles)
    def _():
        start(tile_in(i + 1, 1 - slot))

    wait(tile_in(i, slot))

    @pl.when(i >= 2)
    def _():
        wait(tile_out(i - 2, slot))

    h = in_buf[slot].reshape(rows, D_MODEL)
    xn = _rms(h, gpre_ref[...]).astype(BF16)
    proj = _dot(xn, win_ref[...])
    gate_branch = proj[:, :W]
    u = proj[:, W:]

    uext_ref[halo:halo + rows, :] = u
    uc = convw_ref[CONV_WIDTH - 1:CONV_WIDTH, :] * u + convb_ref[...]
    for j in range(1, CONV_WIDTH):
        start_row = halo - j * batch
        uc = uc + convw_ref[CONV_WIDTH - 1 - j:CONV_WIDTH - j, :] * uext_ref[start_row:start_row + rows, :]
    uext_ref[0:halo, :] = u[rows - halo:rows, :]

    ucb = uc.astype(BF16)
    gates = [_dot(ucb[:, n * LRU_BLOCK:(n + 1) * LRU_BLOCK], wgate_ref[n]) for n in range(LRU_BLOCKS)]
    ga = jnp.concatenate([g[:, :LRU_BLOCK] for g in gates], axis=1)
    gx = jnp.concatenate([g[:, LRU_BLOCK:] for g in gates], axis=1)
    tr = jnp.tanh(ga + ba_ref[...])
    ti = jnp.tanh(gx + bx_ref[...])
    half_c = (-0.5 * LRU_C) * jax.nn.log_sigmoid(lam_ref[...])
    neg_log_a = tr * half_c + half_c
    a = jnp.exp2(neg_log_a * -LOG2E)
    gain2 = jnp.tanh(neg_log_a) * (a * a + 1.0)
    gain = gain2 * lax.rsqrt(jnp.maximum(gain2, jnp.finfo(F32).tiny))
    b2 = gain * uc * (ti + 1.0)

    carry = hc_ref[...]
    hs = []
    for t in range(steps):
        carry = a[t * batch:(t + 1) * batch, :] * carry + b2[t * batch:(t + 1) * batch, :]
        hs.append(carry)
    hc_ref[...] = carry
    hseq2 = jnp.concatenate(hs, axis=0)

    x = gate_branch
    inner = x * (GELU_K1 * (x * x) + GELU_K0)
    y = (hseq2 * (0.25 * x) * (jnp.tanh(inner) + 1.0)).astype(BF16)
    out = h + _rms(_dot(y, wout_ref[...]), gpost_ref[...])
    out_buf[slot] = out.reshape(steps, batch, D_MODEL)
    start(tile_out(i, slot))

    @pl.when(i == n_tiles - 1)
    def _():
        if n_tiles >= 2:
            wait(tile_out(i - 1, 1 - slot))
        wait(tile_out(i, slot))


def _lru_layer(h, g_pre, g_post, w_in, conv_w, conv_b, w_gate_a, b_gate_a, w_gate_x, b_gate_x,
               lam, w_out):
    B, S, _ = h.shape
    W = LRU_WIDTH
    assert B % SUBLANES == 0, "a time step must fill whole sublane groups"
    steps = min(LRU_ROWS // B, S)
    n_tiles = S // steps
    halo = (CONV_WIDTH - 1) * B
    w_gate = (0.5 * jnp.concatenate([w_gate_a, w_gate_x], axis=-1)).astype(BF16)
    hbm = pl.BlockSpec(memory_space=pl.ANY)
    row = lambda v: v.astype(F32).reshape(1, W)
    return pl.pallas_call(
        functools.partial(_lru_kernel, steps=steps, batch=B, n_tiles=n_tiles),
        grid=(n_tiles,),
        in_specs=[hbm, _const_spec((1, D_MODEL)), _const_spec((1, D_MODEL)),
                  _const_spec((D_MODEL, 2 * W)), _const_spec((CONV_WIDTH, W)), _const_spec((1, W)),
                  _const_spec((LRU_BLOCKS, LRU_BLOCK, 2 * LRU_BLOCK)), _const_spec((1, W)),
                  _const_spec((1, W)), _const_spec((1, W)), _const_spec((W, D_MODEL))],
        out_specs=hbm,
        out_shape=jax.ShapeDtypeStruct((B, S, D_MODEL), F32),
        scratch_shapes=[pltpu.VMEM((2, steps, B, D_MODEL), F32), pltpu.VMEM((2, steps, B, D_MODEL), F32),
                        pltpu.SemaphoreType.DMA((2,)), pltpu.SemaphoreType.DMA((2,)),
                        pltpu.VMEM((steps * B + halo, W), F32), pltpu.VMEM((B, W), F32)],
        compiler_params=pltpu.CompilerParams(
            dimension_semantics=("arbitrary",), vmem_limit_bytes=VMEM_LIMIT_BYTES),
        name="rglru",
    )(h, g_pre.reshape(1, D_MODEL), g_post.reshape(1, D_MODEL),
      w_in.astype(BF16), conv_w.astype(F32), row(conv_b), w_gate, row(0.5 * b_gate_a),
      row(0.5 * b_gate_x), row(lam), w_out.astype(BF16))


def kernel(x, ml_w_in, ml_b_if, ml_head_norm, ml_w_out, lru_w_in, lru_conv_w, lru_conv_b, lru_w_gate_a, lru_b_gate_a, lru_w_gate_x, lru_b_gate_x, lru_lambda, lru_w_out, norm_pre_mix, norm_post_mix, norm_pre_ffn, norm_post_ffn, ffn_w_gate, ffn_w_up, ffn_w_down):
    B, S, D = x.shape
    depth = norm_pre_mix.shape[0]
    h = x
    for layer in range(depth):
        j = layer // 2
        if layer % 2 == 0:
            h = _mlstm_layer(h, j, norm_pre_mix[layer], norm_post_mix[layer], ml_w_in, ml_b_if[j],
                             ml_head_norm[j], ml_w_out)
        else:
            h = _lru_layer(h, norm_pre_mix[layer], norm_post_mix[layer], lru_w_in[j], lru_conv_w[j],
                           lru_conv_b[j], lru_w_gate_a[j], lru_b_gate_a[j], lru_w_gate_x[j],
                           lru_b_gate_x[j], lru_lambda[j], lru_w_out[j])
        h = _ffn_layer(h.reshape(B * S, D), layer, norm_pre_ffn[layer], norm_post_ffn[layer],
                       ffn_w_gate, ffn_w_up, ffn_w_down).reshape(B, S, D)
    return h
```

```python
import functools
import math

import jax
import jax.numpy as jnp
from jax import lax
from jax.experimental import pallas as pl
from jax.experimental.pallas import tpu as pltpu

D_MODEL = 1024
ML_HEADS = 8
ML_QK_DIM = 64
ML_V_DIM = 128
ML_QK = ML_HEADS * ML_QK_DIM
ML_V = ML_HEADS * ML_V_DIM
GATE_CAP = 15.0
ML_M_INIT = -1e30
LRU_WIDTH = D_MODEL
LRU_BLOCKS = 4
LRU_BLOCK = LRU_WIDTH // LRU_BLOCKS
CONV_WIDTH = 4
LRU_C = 8.0
D_FF = 2816
EPS = 1e-6
LOG2E = math.log2(math.e)
GELU_K0 = math.sqrt(2.0 / math.pi)
GELU_K1 = GELU_K0 * 0.044715

SUBLANES = 8
LANES = 128
V7X_VMEM_BYTES = 64 * 1024 * 1024
VMEM_LIMIT_BYTES = V7X_VMEM_BYTES * 7 // 8

ML_CHUNK = 128
ML_TILE = 512
LRU_ROWS = 1024
FFN_TILE = 512
FFN_CHUNK = 256

F32 = jnp.float32
BF16 = jnp.bfloat16


def _rms(x, g):
    return x * lax.rsqrt(jnp.mean(x * x, axis=-1, keepdims=True) + EPS) * g


def _sigmoid(x):
    return 0.5 * jnp.tanh(0.5 * x) + 0.5


def _dot(a, b):
    return jnp.dot(a, b, preferred_element_type=F32)


def _dot_nt(a, b):
    return lax.dot_general(a, b, (((1,), (1,)), ((), ())), preferred_element_type=F32)


def _const_spec(shape):
    zeros = (0,) * len(shape)
    return pl.BlockSpec(shape, lambda *_: zeros, pipeline_mode=pl.Buffered(1))


def _layer_spec(stacked, layer):
    shape = stacked.shape[1:]
    zeros = (0,) * len(shape)
    return pl.BlockSpec((None,) + shape, lambda *_: (layer,) + zeros, pipeline_mode=pl.Buffered(1))


def _ffn_kernel(h_hbm, gpre_ref, gpost_ref, wg_ref, wu_ref, wd_ref, o_hbm, in_buf, out_buf, in_sem,
                out_sem, *, fc, tile, n_tiles):
    def tile_in(t, slot):
        rows = pl.ds(pl.multiple_of(t * tile, tile), tile)
        return pltpu.make_async_copy(h_hbm.at[rows, :], in_buf.at[slot], in_sem.at[slot])

    def tile_out(t, slot):
        rows = pl.ds(pl.multiple_of(t * tile, tile), tile)
        return pltpu.make_async_copy(out_buf.at[slot], o_hbm.at[rows, :], out_sem.at[slot])

    tile_in(0, 0).start()

    def one_tile(t, carry):
        slot = lax.rem(t, 2)

        @pl.when(t + 1 < n_tiles)
        def _():
            tile_in(t + 1, 1 - slot).start()

        tile_in(t, slot).wait()

        @pl.when(t >= 2)
        def _():
            tile_out(t - 2, slot).wait()

        h = in_buf[slot]
        xn = _rms(h, gpre_ref[...])
        acc = None
        for c in range(D_FF // fc):
            cols = slice(c * fc, (c + 1) * fc)
            half_g = 0.5 * _dot(xn, wg_ref[:, cols])
            a = (jnp.tanh(half_g) * half_g + half_g) * _dot(xn, wu_ref[:, cols])
            part = _dot(a, wd_ref[cols, :])
            acc = part if acc is None else acc + part
        out_buf[slot] = h + _rms(acc, gpost_ref[...])
        tile_out(t, slot).start()
        return carry

    lax.fori_loop(0, n_tiles, one_tile, 0)
    if n_tiles >= 2:
        tile_out(n_tiles - 2, (n_tiles - 2) % 2).wait()
    tile_out(n_tiles - 1, (n_tiles - 1) % 2).wait()


def _ffn_layer(h, layer, g_pre, g_post, w_gate, w_up, w_down):
    n_tok = h.shape[0]
    tile = min(FFN_TILE, n_tok)
    n_tiles = n_tok // tile
    hbm = pl.BlockSpec(memory_space=pl.ANY)
    return pl.pallas_call(
        functools.partial(_ffn_kernel, fc=FFN_CHUNK, tile=tile, n_tiles=n_tiles),
        grid=(1,),
        in_specs=[hbm, _const_spec((1, D_MODEL)), _const_spec((1, D_MODEL)),
                  _layer_spec(w_gate, layer), _layer_spec(w_up, layer), _layer_spec(w_down, layer)],
        out_specs=hbm,
        out_shape=jax.ShapeDtypeStruct((n_tok, D_MODEL), F32),
        scratch_shapes=[pltpu.VMEM((2, tile, D_MODEL), F32), pltpu.VMEM((2, tile, D_MODEL), F32),
                        pltpu.SemaphoreType.DMA((2,)), pltpu.SemaphoreType.DMA((2,))],
        compiler_params=pltpu.CompilerParams(
            dimension_semantics=("arbitrary",), vmem_limit_bytes=VMEM_LIMIT_BYTES),
        name="swiglu",
    )(h, g_pre.reshape(1, D_MODEL), g_post.reshape(1, D_MODEL), w_gate, w_up, w_down)


def _soft_cap(z):
    return GATE_CAP * jnp.tanh(z / GATE_CAP)


def _lane_prefix(x, combine, identity):
    n = x.shape[-1]
    lane = lax.broadcasted_iota(jnp.int32, x.shape, x.ndim - 1)
    d = 1
    while d < n:
        x = combine(x, jnp.where(lane >= d, pltpu.roll(x, d, axis=x.ndim - 1), identity))
        d *= 2
    return x


def _mlstm_step(hx_ref, hy_ref, gpre_ref, gpost_ref, win_ref, wkg_ref, brow_ref, hnorm_ref, wout_ref,
                write, read, o_ref, c_ref, m_ref, hs_ref, *, tile, chunk):
    H, dk, dv, L = ML_HEADS, ML_QK_DIM, ML_V_DIM, chunk
    n_chunks = tile // L
    log2_q_scale = -0.5 * math.log2(dk)
    q_w, v_w, gate_w, kt_w, prt_w, rr2_w, ea_w, dec_w = write
    q_r, v_r, gate_r, kt_r, prt_r, rr2_r, ea_r, dec_r = read

    causal = (lax.broadcasted_iota(jnp.int32, (L, L), 1)
              <= lax.broadcasted_iota(jnp.int32, (L, L), 0))
    ones_aug = jnp.ones((L, dv), F32)
    zeros_kk = jnp.zeros((dk, dk), F32)

    def scores(c):
        r0 = c * L
        return [_dot(q_r[hh, r0:r0 + L, :], kt_r[hh * dk:(hh + 1) * dk, r0:r0 + L]) for hh in range(H)]

    def state_matmul(c, hh, s_mat):
        r0 = c * L
        q_h = q_r[hh, r0:r0 + L, :]
        kt_h = kt_r[hh * dk:(hh + 1) * dk, r0:r0 + L]
        v_h = v_r[r0:r0 + L, hh * dv:(hh + 1) * dv]
        cm2 = prt_r[c, :, hh:hh + 1]
        w_inter = prt_r[c, :, H + hh:H + hh + 1]
        c_prev = c_ref[hh]
        p_mat = jnp.where(causal, jnp.exp2(cm2 + rr2_r[hh:hh + 1, r0:r0 + L]) * s_mat, 0.0)
        lhs = jnp.concatenate(
            [jnp.concatenate([p_mat, q_h * w_inter], axis=1),
             jnp.concatenate([kt_h * ea_r[hh:hh + 1, r0:r0 + L], zeros_kk], axis=1)], axis=0)
        rhs = jnp.concatenate([jnp.concatenate([v_h, ones_aug], axis=1), c_prev], axis=0)
        r = _dot(lhs, rhs)
        c_ref[hh] = dec_r[c, hh:hh + 1, :] * c_prev + r[L:L + dk]
        return r

    def epilogue(c, hh, r):
        r0 = c * L
        emt = prt_r[c, :, 2 * H + hh:2 * H + hh + 1]
        den = jnp.maximum(jnp.abs(r[0:L, dv:2 * dv]), emt)
        h_out = r[0:L, 0:dv] / den
        hn = h_out * lax.rsqrt(jnp.mean(h_out * h_out, axis=-1, keepdims=True) + EPS)
        hs_ref[r0:r0 + L, hh * dv:(hh + 1) * dv] = hn * gate_r[r0:r0 + L, hh * dv:(hh + 1) * dv]

    PW = 2 * LANES
    gate_rows = []

    def kg_piece(t0):
        def run():
            kg = _dot_nt(wkg_ref[...], xn[t0:t0 + PW, :])
            kt_w[:, t0:t0 + PW] = kg[0:ML_QK, :]
            gate_rows.append(_soft_cap(kg[ML_QK:ML_QK + 2 * H, :] + brow_ref[...]))
        return run

    def q_piece(c0):
        def run():
            q4 = _dot(xn, win_ref[:, c0:c0 + PW])
            for j in range(PW // dk):
                q_w[c0 // dk + j] = q4[:, j * dk:(j + 1) * dk]
        return run

    def v_piece(c0):
        def run():
            v_w[:, c0:c0 + PW] = _dot(xn, win_ref[:, 2 * ML_QK + c0:2 * ML_QK + c0 + PW])
        return run

    def gate_piece(c0):
        def run():
            o = _dot(xn, win_ref[:, 2 * ML_QK + ML_V + c0:2 * ML_QK + ML_V + c0 + PW])
            gate_w[:, c0:c0 + PW] = _sigmoid(o) * hnorm_ref[:, c0:c0 + PW]
        return run

    def bookkeeping():
        grow = jnp.concatenate(gate_rows, axis=1)
        li_all = grow[0:H, :]
        lf_all = jax.nn.log_sigmoid(grow[H:2 * H, :])
        m_prev = m_ref[...]
        for c in range(n_chunks):
            li = li_all[:, c * L:(c + 1) * L]
            b = _lane_prefix(lf_all[:, c * L:(c + 1) * L], jnp.add, 0.0)
            rr = li - b
            rr_max = _lane_prefix(rr, jnp.maximum, -jnp.inf)
            g_tot = b[:, L - 1:L]
            m_t = b + jnp.maximum(rr_max, m_prev)
            m_new = jnp.maximum(g_tot + m_prev, g_tot + rr_max[:, L - 1:L])
            dec_w[c] = jnp.exp(g_tot + m_prev - m_new)
            ea_w[:, c * L:(c + 1) * L] = jnp.exp(g_tot + rr - m_new)
            rr2_w[:, c * L:(c + 1) * L] = rr * LOG2E
            per_row = jnp.concatenate(
                [(b - m_t) * LOG2E + log2_q_scale, jnp.exp(b + m_prev - m_t) * 2.0 ** log2_q_scale,
                 jnp.exp(-m_t), jnp.zeros((L - 3 * H, L), F32)], axis=0)
            prt_w[c] = per_row.T
            m_prev = m_new
        m_ref[...] = m_prev

    pieces = ([kg_piece(t0) for t0 in range(0, tile, PW)] + [q_piece(c0) for c0 in range(0, ML_QK, PW)]
              + [v_piece(c0) for c0 in range(0, ML_V, PW)] + [gate_piece(c0) for c0 in range(0, ML_V, PW)])
    slots = n_chunks * 3
    per_slot = -(-len(pieces) // slots)

    def issue_pieces():
        for _ in range(per_slot):
            if pieces:
                pieces.pop(0)()

    s_mat = scores(0)
    xn = _rms(hx_ref[...], gpre_ref[...])
    for c in range(n_chunks):
        res = {}
        for hh in range(H):
            res[hh] = state_matmul(c, hh, s_mat[hh])
            if hh in (1, 4, 7):
                issue_pieces()
                for done in (hh - 2, hh - 1, hh) if hh > 1 else (0, 1):
                    epilogue(c, done, res[done])
        if c + 1 < n_chunks:
            s_mat = scores(c + 1)
    while pieces:
        pieces.pop(0)()
    bookkeeping()

    y = _dot(hs_ref[...], wout_ref[...])
    o_ref[...] = hy_ref[...] + _rms(y, gpost_ref[...])


def _mlstm_kernel(hx_ref, hy_ref, gpre_ref, gpost_ref, win_ref, brow_ref, hnorm_ref,
                  wout_ref, o_ref, c_ref, m_ref, hs_ref, wkg_ref, *proj_refs, tile, chunk, tiles_per_seq):
    s = pl.program_id(0)
    n_buf = len(proj_refs) // 2
    bufs = (proj_refs[:n_buf], proj_refs[n_buf:])

    @pl.when(s == 0)
    def _():
        for ref in bufs[1]:
            ref[...] = jnp.zeros_like(ref)
        wkg_ref[0:ML_QK, :] = win_ref[:, ML_QK:2 * ML_QK].T
        gate_cols = jnp.concatenate(
            [win_ref[:, 2 * ML_QK + 2 * ML_V:], jnp.zeros((D_MODEL, LANES - 2 * ML_HEADS), F32)], axis=1)
        wkg_ref[ML_QK:ML_QK + 2 * ML_HEADS, :] = gate_cols.T[0:2 * ML_HEADS, :]

    @pl.when(s % tiles_per_seq == 0)
    def _():
        m_ref[...] = jnp.full_like(m_ref, ML_M_INIT)

    @pl.when(jnp.maximum(s - 1, 0) % tiles_per_seq == 0)
    def _():
        c_ref[...] = jnp.zeros_like(c_ref)

    step = functools.partial(
        _mlstm_step, hx_ref, hy_ref, gpre_ref, gpost_ref, win_ref, wkg_ref, brow_ref, hnorm_ref,
        wout_ref, o_ref=o_ref, c_ref=c_ref, m_ref=m_ref, hs_ref=hs_ref, tile=tile, chunk=chunk)
    pl.when(s % 2 == 0)(lambda: step(bufs[0], bufs[1]))
    pl.when(s % 2 == 1)(lambda: step(bufs[1], bufs[0]))


def _mlstm_layer(h, layer, g_pre, g_post, w_in, b_if, head_norm, w_out):
    B, S, _ = h.shape
    H = ML_HEADS
    tile = min(ML_TILE, S)
    chunk = min(ML_CHUNK, tile)
    tiles_per_seq = S // tile
    n_tiles = B * tiles_per_seq
    b_row = b_if.astype(F32).reshape(2 * H, 1)

    def ahead(s):
        k = jnp.minimum(s, n_tiles - 1)
        return (k // tiles_per_seq, k % tiles_per_seq, 0)

    def behind(s):
        k = jnp.maximum(s - 1, 0)
        return (k // tiles_per_seq, k % tiles_per_seq, 0)

    block = (None, tile, D_MODEL)
    n_chunks = tile // chunk
    stage_bufs = [pltpu.VMEM((H, tile, ML_QK_DIM), F32),
                  pltpu.VMEM((tile, ML_V), F32),
                  pltpu.VMEM((tile, ML_V), F32),
                  pltpu.VMEM((ML_QK, tile), F32),
                  pltpu.VMEM((n_chunks, chunk, LANES), F32),
                  pltpu.VMEM((H, tile), F32),
                  pltpu.VMEM((H, tile), F32),
                  pltpu.VMEM((n_chunks, H, 1), F32)]
    return pl.pallas_call(
        functools.partial(_mlstm_kernel, tile=tile, chunk=chunk, tiles_per_seq=tiles_per_seq),
        grid=(n_tiles + 1,),
        in_specs=[pl.BlockSpec(block, ahead), pl.BlockSpec(block, behind),
                  _const_spec((1, D_MODEL)), _const_spec((1, D_MODEL)),
                  _layer_spec(w_in, layer), _const_spec((2 * H, 1)),
                  _const_spec((1, ML_V)), _layer_spec(w_out, layer)],
        out_specs=pl.BlockSpec(block, behind),
        out_shape=jax.ShapeDtypeStruct((B, S, D_MODEL), F32),
        scratch_shapes=[pltpu.VMEM((H, ML_QK_DIM, 2 * ML_V_DIM), F32),
                        pltpu.VMEM((H, 1), F32),
                        pltpu.VMEM((tile, ML_V), F32),
                        pltpu.VMEM((ML_QK + 2 * H, D_MODEL), F32)] + stage_bufs + stage_bufs,
        compiler_params=pltpu.CompilerParams(
            dimension_semantics=("arbitrary",), vmem_limit_bytes=VMEM_LIMIT_BYTES),
        name="mlstm",
    )(h, h, g_pre.reshape(1, D_MODEL), g_post.reshape(1, D_MODEL), w_in,
      b_row, head_norm.reshape(1, ML_V), w_out)


def _lru_kernel(h_hbm, gpre_ref, gpost_ref, win_ref, convw_ref, convb_ref, wgate_ref, ba_ref,
                bx_ref, lam_ref, wout_ref, o_hbm, in_buf, out_buf, in_sem, out_sem, uext_ref, hc_ref,
                *, steps, batch, n_tiles):
    W = LRU_WIDTH
    rows = steps * batch
    halo = (CONV_WIDTH - 1) * batch
    i = pl.program_id(0)
    slot = lax.rem(i, 2)

    def tile_in(tile, to_slot):
        t0 = pl.multiple_of(tile * steps, steps)
        return [pltpu.make_async_copy(h_hbm.at[b, pl.ds(t0, steps), :], in_buf.at[to_slot, :, b, :],
                                      in_sem.at[to_slot]) for b in range(batch)]

    def tile_out(tile, from_slot):
        t0 = pl.multiple_of(tile * steps, steps)
        return [pltpu.make_async_copy(out_buf.at[from_slot, :, b, :], o_hbm.at[b, pl.ds(t0, steps), :],
                                      out_sem.at[from_slot]) for b in range(batch)]

    def start(copies):
        for c in copies:
            c.start()

    def wait(copies):
        for c in copies:
            c.wait()

    @pl.when(i == 0)
    def _():
        uext_ref[0:halo, :] = jnp.zeros((halo, W), F32)
        hc_ref[...] = jnp.zeros_like(hc_ref)
        start(tile_in(0, 0))

    @pl.when(i + 1 < n_tiles)
    def _():
        start(tile_in(i + 1, 1 - slot))

    wait(tile_in(i, slot))

    @pl.when(i >= 2)
    def _():
        wait(tile_out(i - 2, slot))

    h = in_buf[slot].reshape(rows, D_MODEL)
    xn = _rms(h, gpre_ref[...]).astype(BF16)
    proj = _dot(xn, win_ref[...])
    gate_branch = proj[:, :W]
    u = proj[:, W:]

    uext_ref[halo:halo + rows, :] = u
    uc = convw_ref[CONV_WIDTH - 1:CONV_WIDTH, :] * u + convb_ref[...]
    for j in range(1, CONV_WIDTH):
        start_row = halo - j * batch
        uc = uc + convw_ref[CONV_WIDTH - 1 - j:CONV_WIDTH - j, :] * uext_ref[start_row:start_row + rows, :]
    uext_ref[0:halo, :] = u[rows - halo:rows, :]

    ucb = uc.astype(BF16)
    gates = [_dot(ucb[:, n * LRU_BLOCK:(n + 1) * LRU_BLOCK], wgate_ref[n]) for n in range(LRU_BLOCKS)]
    ga = jnp.concatenate([g[:, :LRU_BLOCK] for g in gates], axis=1)
    gx = jnp.concatenate([g[:, LRU_BLOCK:] for g in gates], axis=1)
    tr = jnp.tanh(ga + ba_ref[...])
    ti = jnp.tanh(gx + bx_ref[...])
    half_c = (-0.5 * LRU_C) * jax.nn.log_sigmoid(lam_ref[...])
    neg_log_a = tr * half_c + half_c
    a = jnp.exp2(neg_log_a * -LOG2E)
    gain2 = jnp.tanh(neg_log_a) * (a * a + 1.0)
    gain = gain2 * lax.rsqrt(jnp.maximum(gain2, jnp.finfo(F32).tiny))
    b2 = gain * uc * (ti + 1.0)

    carry = hc_ref[...]
    hs = []
    for t in range(steps):
        carry = a[t * batch:(t + 1) * batch, :] * carry + b2[t * batch:(t + 1) * batch, :]
        hs.append(carry)
    hc_ref[...] = carry
    hseq2 = jnp.concatenate(hs, axis=0)

    x = gate_branch
    inner = x * (GELU_K1 * (x * x) + GELU_K0)
    y = (hseq2 * (0.25 * x) * (jnp.tanh(inner) + 1.0)).astype(BF16)
    out = h + _rms(_dot(y, wout_ref[...]), gpost_ref[...])
    out_buf[slot] = out.reshape(steps, batch, D_MODEL)
    start(tile_out(i, slot))

    @pl.when(i == n_tiles - 1)
    def _():
        if n_tiles >= 2:
            wait(tile_out(i - 1, 1 - slot))
        wait(tile_out(i, slot))


def _lru_layer(h, g_pre, g_post, w_in, conv_w, conv_b, w_gate_a, b_gate_a, w_gate_x, b_gate_x,
               lam, w_out):
    B, S, _ = h.shape
    W = LRU_WIDTH
    assert B % SUBLANES == 0, "a time step must fill whole sublane groups"
    steps = min(LRU_ROWS // B, S)
    n_tiles = S // steps
    halo = (CONV_WIDTH - 1) * B
    w_gate = (0.5 * jnp.concatenate([w_gate_a, w_gate_x], axis=-1)).astype(BF16)
    hbm = pl.BlockSpec(memory_space=pl.ANY)
    row = lambda v: v.astype(F32).reshape(1, W)
    return pl.pallas_call(
        functools.partial(_lru_kernel, steps=steps, batch=B, n_tiles=n_tiles),
        grid=(n_tiles,),
        in_specs=[hbm, _const_spec((1, D_MODEL)), _const_spec((1, D_MODEL)),
                  _const_spec((D_MODEL, 2 * W)), _const_spec((CONV_WIDTH, W)), _const_spec((1, W)),
                  _const_spec((LRU_BLOCKS, LRU_BLOCK, 2 * LRU_BLOCK)), _const_spec((1, W)),
                  _const_spec((1, W)), _const_spec((1, W)), _const_spec((W, D_MODEL))],
        out_specs=hbm,
        out_shape=jax.ShapeDtypeStruct((B, S, D_MODEL), F32),
        scratch_shapes=[pltpu.VMEM((2, steps, B, D_MODEL), F32), pltpu.VMEM((2, steps, B, D_MODEL), F32),
                        pltpu.SemaphoreType.DMA((2,)), pltpu.SemaphoreType.DMA((2,)),
                        pltpu.VMEM((steps * B + halo, W), F32), pltpu.VMEM((B, W), F32)],
        compiler_params=pltpu.CompilerParams(
            dimension_semantics=("arbitrary",), vmem_limit_bytes=VMEM_LIMIT_BYTES),
        name="rglru",
    )(h, g_pre.reshape(1, D_MODEL), g_post.reshape(1, D_MODEL),
      w_in.astype(BF16), conv_w.astype(F32), row(conv_b), w_gate, row(0.5 * b_gate_a),
      row(0.5 * b_gate_x), row(lam), w_out.astype(BF16))


def kernel(x, ml_w_in, ml_b_if, ml_head_norm, ml_w_out, lru_w_in, lru_conv_w, lru_conv_b, lru_w_gate_a, lru_b_gate_a, lru_w_gate_x, lru_b_gate_x, lru_lambda, lru_w_out, norm_pre_mix, norm_post_mix, norm_pre_ffn, norm_post_ffn, ffn_w_gate, ffn_w_up, ffn_w_down):
    B, S, D = x.shape
    depth = norm_pre_mix.shape[0]
    h = x
    for layer in range(depth):
        j = layer // 2
        if layer % 2 == 0:
            h = _mlstm_layer(h, j, norm_pre_mix[layer], norm_post_mix[layer], ml_w_in, ml_b_if[j],
                             ml_head_norm[j], ml_w_out)
        else:
            h = _lru_layer(h, norm_pre_mix[layer], norm_post_mix[layer], lru_w_in[j], lru_conv_w[j],
                           lru_conv_b[j], lru_w_gate_a[j], lru_b_gate_a[j], lru_w_gate_x[j],
                           lru_b_gate_x[j], lru_lambda[j], lru_w_out[j])
        h = _ffn_layer(h.reshape(B * S, D), layer, norm_pre_ffn[layer], norm_post_ffn[layer],
                       ffn_w_gate, ffn_w_up, ffn_w_down).reshape(B, S, D)
    return h
```

```python
import functools
import math

import jax
import jax.numpy as jnp
from jax import lax
from jax.experimental import pallas as pl
from jax.experimental.pallas import tpu as pltpu

D_MODEL = 1024
ML_HEADS = 8
ML_QK_DIM = 64
ML_V_DIM = 128
ML_QK = ML_HEADS * ML_QK_DIM
ML_V = ML_HEADS * ML_V_DIM
GATE_CAP = 15.0
ML_M_INIT = -1e30
LRU_WIDTH = D_MODEL
LRU_BLOCKS = 4
LRU_BLOCK = LRU_WIDTH // LRU_BLOCKS
CONV_WIDTH = 4
LRU_C = 8.0
D_FF = 2816
EPS = 1e-6
LOG2E = math.log2(math.e)
GELU_K0 = math.sqrt(2.0 / math.pi)
GELU_K1 = GELU_K0 * 0.044715

SUBLANES = 8
LANES = 128
V7X_VMEM_BYTES = 64 * 1024 * 1024
VMEM_LIMIT_BYTES = V7X_VMEM_BYTES * 7 // 8

ML_CHUNK = 128
ML_TILE = 512
LRU_ROWS = 1024
FFN_TILE = 512
FFN_CHUNK = 256

F32 = jnp.float32
BF16 = jnp.bfloat16


def _rms(x, g):
    return x * lax.rsqrt(jnp.mean(x * x, axis=-1, keepdims=True) + EPS) * g


def _sigmoid(x):
    return 0.5 * jnp.tanh(0.5 * x) + 0.5


def _dot(a, b):
    return jnp.dot(a, b, preferred_element_type=F32)


def _dot_nt(a, b):
    return lax.dot_general(a, b, (((1,), (1,)), ((), ())), preferred_element_type=F32)


def _const_spec(shape):
    zeros = (0,) * len(shape)
    return pl.BlockSpec(shape, lambda *_: zeros, pipeline_mode=pl.Buffered(1))


def _layer_spec(stacked, layer):
    shape = stacked.shape[1:]
    zeros = (0,) * len(shape)
    return pl.BlockSpec((None,) + shape, lambda *_: (layer,) + zeros, pipeline_mode=pl.Buffered(1))


def _ffn_kernel(h_ref, gpre_ref, gpost_ref, wg_ref, wu_ref, wd_ref, o_ref, *, fc):
    h = h_ref[...]
    xn = _rms(h, gpre_ref[...])
    acts = []
    for c in range(D_FF // fc):
        cols = slice(c * fc, (c + 1) * fc)
        half_g = 0.5 * _dot(xn, wg_ref[:, cols])
        acts.append((jnp.tanh(half_g) * half_g + half_g) * _dot(xn, wu_ref[:, cols]))
    y = _dot(jnp.concatenate(acts, axis=1), wd_ref[...])
    o_ref[...] = h + _rms(y, gpost_ref[...])


def _ffn_layer(h, layer, g_pre, g_post, w_gate, w_up, w_down):
    n_tok = h.shape[0]
    tile = min(FFN_TILE, n_tok)
    tok_spec = pl.BlockSpec((tile, D_MODEL), lambda i: (i, 0))
    return pl.pallas_call(
        functools.partial(_ffn_kernel, fc=FFN_CHUNK),
        grid=(n_tok // tile,),
        in_specs=[tok_spec, _const_spec((1, D_MODEL)), _const_spec((1, D_MODEL)),
                  _layer_spec(w_gate, layer), _layer_spec(w_up, layer), _layer_spec(w_down, layer)],
        out_specs=tok_spec,
        out_shape=jax.ShapeDtypeStruct((n_tok, D_MODEL), F32),
        compiler_params=pltpu.CompilerParams(
            dimension_semantics=("parallel",), vmem_limit_bytes=VMEM_LIMIT_BYTES),
        name="swiglu",
    )(h, g_pre.reshape(1, D_MODEL), g_post.reshape(1, D_MODEL), w_gate, w_up, w_down)


def _soft_cap(z):
    return GATE_CAP * jnp.tanh(z / GATE_CAP)


def _lane_prefix(x, combine, identity):
    n = x.shape[-1]
    lane = lax.broadcasted_iota(jnp.int32, x.shape, x.ndim - 1)
    d = 1
    while d < n:
        x = combine(x, jnp.where(lane >= d, pltpu.roll(x, d, axis=x.ndim - 1), identity))
        d *= 2
    return x


def _mlstm_step(hx_ref, hy_ref, gpre_ref, gpost_ref, win_ref, wkg_ref, brow_ref, hnorm_ref, wout_ref,
                write, read, o_ref, c_ref, m_ref, hs_ref, *, tile, chunk):
    H, dk, dv, L = ML_HEADS, ML_QK_DIM, ML_V_DIM, chunk
    n_chunks = tile // L
    log2_q_scale = -0.5 * math.log2(dk)
    q_w, v_w, gate_w, kt_w, prt_w, rr2_w, ea_w, dec_w = write
    q_r, v_r, gate_r, kt_r, prt_r, rr2_r, ea_r, dec_r = read

    causal = (lax.broadcasted_iota(jnp.int32, (L, L), 1)
              <= lax.broadcasted_iota(jnp.int32, (L, L), 0))
    ones_aug = jnp.ones((L, dv), F32)
    zeros_kk = jnp.zeros((dk, dk), F32)

    def scores(c):
        r0 = c * L
        return [_dot(q_r[hh, r0:r0 + L, :], kt_r[hh * dk:(hh + 1) * dk, r0:r0 + L]) for hh in range(H)]

    def state_matmul(c, hh, s_mat):
        r0 = c * L
        q_h = q_r[hh, r0:r0 + L, :]
        kt_h = kt_r[hh * dk:(hh + 1) * dk, r0:r0 + L]
        v_h = v_r[r0:r0 + L, hh * dv:(hh + 1) * dv]
        cm2 = prt_r[c, :, hh:hh + 1]
        w_inter = prt_r[c, :, H + hh:H + hh + 1]
        c_prev = c_ref[hh]
        p_mat = jnp.where(causal, jnp.exp2(cm2 + rr2_r[hh:hh + 1, r0:r0 + L]) * s_mat, 0.0)
        lhs = jnp.concatenate(
            [jnp.concatenate([p_mat, q_h * w_inter], axis=1),
             jnp.concatenate([kt_h * ea_r[hh:hh + 1, r0:r0 + L], zeros_kk], axis=1)], axis=0)
        rhs = jnp.concatenate([jnp.concatenate([v_h, ones_aug], axis=1), c_prev], axis=0)
        r = _dot(lhs, rhs)
        c_ref[hh] = dec_r[c, hh:hh + 1, :] * c_prev + r[L:L + dk]
        return r

    def epilogue(c, hh, r):
        r0 = c * L
        emt = prt_r[c, :, 2 * H + hh:2 * H + hh + 1]
        den = jnp.maximum(jnp.abs(r[0:L, dv:2 * dv]), emt)
        h_out = r[0:L, 0:dv] / den
        hn = h_out * lax.rsqrt(jnp.mean(h_out * h_out, axis=-1, keepdims=True) + EPS)
        hs_ref[r0:r0 + L, hh * dv:(hh + 1) * dv] = hn * gate_r[r0:r0 + L, hh * dv:(hh + 1) * dv]

    PW = 2 * LANES
    gate_rows = []

    def kg_piece(t0):
        def run():
            kg = _dot_nt(wkg_ref[...], xn[t0:t0 + PW, :])
            kt_w[:, t0:t0 + PW] = kg[0:ML_QK, :]
            gate_rows.append(_soft_cap(kg[ML_QK:ML_QK + 2 * H, :] + brow_ref[...]))
        return run

    def q_piece(c0):
        def run():
            q4 = _dot(xn, win_ref[:, c0:c0 + PW])
            for j in range(PW // dk):
                q_w[c0 // dk + j] = q4[:, j * dk:(j + 1) * dk]
        return run

    def v_piece(c0):
        def run():
            v_w[:, c0:c0 + PW] = _dot(xn, win_ref[:, 2 * ML_QK + c0:2 * ML_QK + c0 + PW])
        return run

    def gate_piece(c0):
        def run():
            o = _dot(xn, win_ref[:, 2 * ML_QK + ML_V + c0:2 * ML_QK + ML_V + c0 + PW])
            gate_w[:, c0:c0 + PW] = _sigmoid(o) * hnorm_ref[:, c0:c0 + PW]
        return run

    def bookkeeping():
        grow = jnp.concatenate(gate_rows, axis=1)
        li_all = grow[0:H, :]
        lf_all = jax.nn.log_sigmoid(grow[H:2 * H, :])
        m_prev = m_ref[...]
        for c in range(n_chunks):
            li = li_all[:, c * L:(c + 1) * L]
            b = _lane_prefix(lf_all[:, c * L:(c + 1) * L], jnp.add, 0.0)
            rr = li - b
            rr_max = _lane_prefix(rr, jnp.maximum, -jnp.inf)
            g_tot = b[:, L - 1:L]
            m_t = b + jnp.maximum(rr_max, m_prev)
            m_new = jnp.maximum(g_tot + m_prev, g_tot + rr_max[:, L - 1:L])
            dec_w[c] = jnp.exp(g_tot + m_prev - m_new)
            ea_w[:, c * L:(c + 1) * L] = jnp.exp(g_tot + rr - m_new)
            rr2_w[:, c * L:(c + 1) * L] = rr * LOG2E
            per_row = jnp.concatenate(
                [(b - m_t) * LOG2E + log2_q_scale, jnp.exp(b + m_prev - m_t) * 2.0 ** log2_q_scale,
                 jnp.exp(-m_t), jnp.zeros((L - 3 * H, L), F32)], axis=0)
            prt_w[c] = per_row.T
            m_prev = m_new
        m_ref[...] = m_prev

    pieces = ([kg_piece(t0) for t0 in range(0, tile, PW)] + [q_piece(c0) for c0 in range(0, ML_QK, PW)]
              + [v_piece(c0) for c0 in range(0, ML_V, PW)] + [gate_piece(c0) for c0 in range(0, ML_V, PW)])
    slots = n_chunks * 3
    per_slot = -(-len(pieces) // slots)

    def issue_pieces():
        for _ in range(per_slot):
            if pieces:
                pieces.pop(0)()

    s_mat = scores(0)
    xn = _rms(hx_ref[...], gpre_ref[...])
    for c in range(n_chunks):
        res = {}
        for hh in range(H):
            res[hh] = state_matmul(c, hh, s_mat[hh])
            if hh in (1, 4, 7):
                issue_pieces()
                for done in (hh - 2, hh - 1, hh) if hh > 1 else (0, 1):
                    epilogue(c, done, res[done])
        if c + 1 < n_chunks:
            s_mat = scores(c + 1)
    while pieces:
        pieces.pop(0)()
    bookkeeping()

    y = _dot(hs_ref[...], wout_ref[...])
    o_ref[...] = hy_ref[...] + _rms(y, gpost_ref[...])


def _mlstm_kernel(hx_ref, hy_ref, gpre_ref, gpost_ref, win_ref, brow_ref, hnorm_ref,
                  wout_ref, o_ref, c_ref, m_ref, hs_ref, wkg_ref, *proj_refs, tile, chunk, tiles_per_seq):
    s = pl.program_id(0)
    n_buf = len(proj_refs) // 2
    bufs = (proj_refs[:n_buf], proj_refs[n_buf:])

    @pl.when(s == 0)
    def _():
        for ref in bufs[1]:
            ref[...] = jnp.zeros_like(ref)
        wkg_ref[0:ML_QK, :] = win_ref[:, ML_QK:2 * ML_QK].T
        gate_cols = jnp.concatenate(
            [win_ref[:, 2 * ML_QK + 2 * ML_V:], jnp.zeros((D_MODEL, LANES - 2 * ML_HEADS), F32)], axis=1)
        wkg_ref[ML_QK:ML_QK + 2 * ML_HEADS, :] = gate_cols.T[0:2 * ML_HEADS, :]

    @pl.when(s % tiles_per_seq == 0)
    def _():
        m_ref[...] = jnp.full_like(m_ref, ML_M_INIT)

    @pl.when(jnp.maximum(s - 1, 0) % tiles_per_seq == 0)
    def _():
        c_ref[...] = jnp.zeros_like(c_ref)

    step = functools.partial(
        _mlstm_step, hx_ref, hy_ref, gpre_ref, gpost_ref, win_ref, wkg_ref, brow_ref, hnorm_ref,
        wout_ref, o_ref=o_ref, c_ref=c_ref, m_ref=m_ref, hs_ref=hs_ref, tile=tile, chunk=chunk)
    pl.when(s % 2 == 0)(lambda: step(bufs[0], bufs[1]))
    pl.when(s % 2 == 1)(lambda: step(bufs[1], bufs[0]))


def _mlstm_layer(h, layer, g_pre, g_post, w_in, b_if, head_norm, w_out):
    B, S, _ = h.shape
    H = ML_HEADS
    tile = min(ML_TILE, S)
    chunk = min(ML_CHUNK, tile)
    tiles_per_seq = S // tile
    n_tiles = B * tiles_per_seq
    b_row = b_if.astype(F32).reshape(2 * H, 1)

    def ahead(s):
        k = jnp.minimum(s, n_tiles - 1)
        return (k // tiles_per_seq, k % tiles_per_seq, 0)

    def behind(s):
        k = jnp.maximum(s - 1, 0)
        return (k // tiles_per_seq, k % tiles_per_seq, 0)

    block = (None, tile, D_MODEL)
    n_chunks = tile // chunk
    stage_bufs = [pltpu.VMEM((H, tile, ML_QK_DIM), F32),
                  pltpu.VMEM((tile, ML_V), F32),
                  pltpu.VMEM((tile, ML_V), F32),
                  pltpu.VMEM((ML_QK, tile), F32),
                  pltpu.VMEM((n_chunks, chunk, LANES), F32),
                  pltpu.VMEM((H, tile), F32),
                  pltpu.VMEM((H, tile), F32),
                  pltpu.VMEM((n_chunks, H, 1), F32)]
    return pl.pallas_call(
        functools.partial(_mlstm_kernel, tile=tile, chunk=chunk, tiles_per_seq=tiles_per_seq),
        grid=(n_tiles + 1,),
        in_specs=[pl.BlockSpec(block, ahead), pl.BlockSpec(block, behind),
                  _const_spec((1, D_MODEL)), _const_spec((1, D_MODEL)),
                  _layer_spec(w_in, layer), _const_spec((2 * H, 1)),
                  _const_spec((1, ML_V)), _layer_spec(w_out, layer)],
        out_specs=pl.BlockSpec(block, behind),
        out_shape=jax.ShapeDtypeStruct((B, S, D_MODEL), F32),
        scratch_shapes=[pltpu.VMEM((H, ML_QK_DIM, 2 * ML_V_DIM), F32),
                        pltpu.VMEM((H, 1), F32),
                        pltpu.VMEM((tile, ML_V), F32),
                        pltpu.VMEM((ML_QK + 2 * H, D_MODEL), F32)] + stage_bufs + stage_bufs,
        compiler_params=pltpu.CompilerParams(
            dimension_semantics=("arbitrary",), vmem_limit_bytes=VMEM_LIMIT_BYTES),
        name="mlstm",
    )(h, h, g_pre.reshape(1, D_MODEL), g_post.reshape(1, D_MODEL), w_in,
      b_row, head_norm.reshape(1, ML_V), w_out)


def _lru_kernel(h_hbm, gpre_ref, gpost_ref, win_ref, convw_ref, convb_ref, wgate_ref, ba_ref,
                bx_ref, lam_ref, wout_ref, o_hbm, in_buf, out_buf, in_sem, out_sem, uext_ref, hc_ref,
                *, steps, batch, n_tiles):
    W = LRU_WIDTH
    rows = steps * batch
    halo = (CONV_WIDTH - 1) * batch
    i = pl.program_id(0)
    slot = lax.rem(i, 2)

    def tile_in(tile, to_slot):
        t0 = pl.multiple_of(tile * steps, steps)
        return [pltpu.make_async_copy(h_hbm.at[b, pl.ds(t0, steps), :], in_buf.at[to_slot, :, b, :],
                                      in_sem.at[to_slot]) for b in range(batch)]

    def tile_out(tile, from_slot):
        t0 = pl.multiple_of(tile * steps, steps)
        return [pltpu.make_async_copy(out_buf.at[from_slot, :, b, :], o_hbm.at[b, pl.ds(t0, steps), :],
                                      out_sem.at[from_slot]) for b in range(batch)]

    def start(copies):
        for c in copies:
            c.start()

    def wait(copies):
        for c in copies:
            c.wait()

    @pl.when(i == 0)
    def _():
        uext_ref[0:halo, :] = jnp.zeros((halo, W), F32)
        hc_ref[...] = jnp.zeros_like(hc_ref)
        start(tile_in(0, 0))

    @pl.when(i + 1 < n_tiles)
    def _():
        start(tile_in(i + 1, 1 - slot))

    wait(tile_in(i, slot))

    @pl.when(i >= 2)
    def _():
        wait(tile_out(i - 2, slot))

    h = in_buf[slot].reshape(rows, D_MODEL)
    xn = _rms(h, gpre_ref[...]).astype(BF16)
    x = _dot(xn, win_ref[:, :W])
    gelu4 = (0.25 * x) * (jnp.tanh(x * (GELU_K1 * (x * x) + GELU_K0)) + 1.0)
    bits = lax.bitcast_convert_type(gelu4, jnp.uint32)
    folded = functools.reduce(jnp.bitwise_or, [bits[:, j:j + LANES] for j in range(0, W, LANES)])
    folded = functools.reduce(jnp.bitwise_or, [folded[r:r + SUBLANES] for r in range(0, rows, SUBLANES)])
    zero = lax.bitcast_convert_type((folded >> 16) >> 16, F32)[0:1, :].astype(BF16)
    n_piece = 4
    pw = W // n_piece
    u_parts = [_dot(xn, win_ref[:, W + p * pw:W + (p + 1) * pw]) for p in range(n_piece - 1)]
    xn_last = xn + jnp.concatenate([zero] * (D_MODEL // LANES), axis=1)
    u_parts.append(_dot(xn_last, win_ref[:, W + (n_piece - 1) * pw:]))
    u = jnp.concatenate(u_parts, axis=1)

    uext_ref[halo:halo + rows, :] = u
    uc = convw_ref[CONV_WIDTH - 1:CONV_WIDTH, :] * u + convb_ref[...]
    for j in range(1, CONV_WIDTH):
        start_row = halo - j * batch
        uc = uc + convw_ref[CONV_WIDTH - 1 - j:CONV_WIDTH - j, :] * uext_ref[start_row:start_row + rows, :]
    uext_ref[0:halo, :] = u[rows - halo:rows, :]

    ucb = uc.astype(BF16)
    gates = [_dot(ucb[:, n * LRU_BLOCK:(n + 1) * LRU_BLOCK], wgate_ref[n]) for n in range(LRU_BLOCKS)]
    ga = jnp.concatenate([g[:, :LRU_BLOCK] for g in gates], axis=1)
    gx = jnp.concatenate([g[:, LRU_BLOCK:] for g in gates], axis=1)
    tr = jnp.tanh(ga + ba_ref[...])
    ti = jnp.tanh(gx + bx_ref[...])
    half_c = (-0.5 * LRU_C) * jax.nn.log_sigmoid(lam_ref[...])
    neg_log_a = tr * half_c + half_c
    a = jnp.exp2(neg_log_a * -LOG2E)
    gain2 = jnp.tanh(neg_log_a) * (a * a + 1.0)
    gain = gain2 * lax.rsqrt(jnp.maximum(gain2, jnp.finfo(F32).tiny))
    b2 = gain * uc * (ti + 1.0)

    carry = hc_ref[...]
    hs = []
    for t in range(steps):
        carry = a[t * batch:(t + 1) * batch, :] * carry + b2[t * batch:(t + 1) * batch, :]
        hs.append(carry)
    hc_ref[...] = carry
    hseq2 = jnp.concatenate(hs, axis=0)

    y = (hseq2 * gelu4).astype(BF16)
    out = h + _rms(_dot(y, wout_ref[...]), gpost_ref[...])
    out_buf[slot] = out.reshape(steps, batch, D_MODEL)
    start(tile_out(i, slot))

    @pl.when(i == n_tiles - 1)
    def _():
        if n_tiles >= 2:
            wait(tile_out(i - 1, 1 - slot))
        wait(tile_out(i, slot))


def _lru_layer(h, g_pre, g_post, w_in, conv_w, conv_b, w_gate_a, b_gate_a, w_gate_x, b_gate_x,
               lam, w_out):
    B, S, _ = h.shape
    W = LRU_WIDTH
    assert B % SUBLANES == 0, "a time step must fill whole sublane groups"
    steps = min(LRU_ROWS // B, S)
    n_tiles = S // steps
    halo = (CONV_WIDTH - 1) * B
    w_gate = (0.5 * jnp.concatenate([w_gate_a, w_gate_x], axis=-1)).astype(BF16)
    hbm = pl.BlockSpec(memory_space=pl.ANY)
    row = lambda v: v.astype(F32).reshape(1, W)
    return pl.pallas_call(
        functools.partial(_lru_kernel, steps=steps, batch=B, n_tiles=n_tiles),
        grid=(n_tiles,),
        in_specs=[hbm, _const_spec((1, D_MODEL)), _const_spec((1, D_MODEL)),
                  _const_spec((D_MODEL, 2 * W)), _const_spec((CONV_WIDTH, W)), _const_spec((1, W)),
                  _const_spec((LRU_BLOCKS, LRU_BLOCK, 2 * LRU_BLOCK)), _const_spec((1, W)),
                  _const_spec((1, W)), _const_spec((1, W)), _const_spec((W, D_MODEL))],
        out_specs=hbm,
        out_shape=jax.ShapeDtypeStruct((B, S, D_MODEL), F32),
        scratch_shapes=[pltpu.VMEM((2, steps, B, D_MODEL), F32), pltpu.VMEM((2, steps, B, D_MODEL), F32),
                        pltpu.SemaphoreType.DMA((2,)), pltpu.SemaphoreType.DMA((2,)),
                        pltpu.VMEM((steps * B + halo, W), F32), pltpu.VMEM((B, W), F32)],
        compiler_params=pltpu.CompilerParams(
            dimension_semantics=("arbitrary",), vmem_limit_bytes=VMEM_LIMIT_BYTES),
        name="rglru",
    )(h, g_pre.reshape(1, D_MODEL), g_post.reshape(1, D_MODEL),
      w_in.astype(BF16), conv_w.astype(F32), row(conv_b), w_gate, row(0.5 * b_gate_a),
      row(0.5 * b_gate_x), row(lam), w_out.astype(BF16))


def kernel(x, ml_w_in, ml_b_if, ml_head_norm, ml_w_out, lru_w_in, lru_conv_w, lru_conv_b, lru_w_gate_a, lru_b_gate_a, lru_w_gate_x, lru_b_gate_x, lru_lambda, lru_w_out, norm_pre_mix, norm_post_mix, norm_pre_ffn, norm_post_ffn, ffn_w_gate, ffn_w_up, ffn_w_down):
    B, S, D = x.shape
    depth = norm_pre_mix.shape[0]
    h = x
    for layer in range(depth):
        j = layer // 2
        if layer % 2 == 0:
            h = _mlstm_layer(h, j, norm_pre_mix[layer], norm_post_mix[layer], ml_w_in, ml_b_if[j],
                             ml_head_norm[j], ml_w_out)
        else:
            h = _lru_layer(h, norm_pre_mix[layer], norm_post_mix[layer], lru_w_in[j], lru_conv_w[j],
                           lru_conv_b[j], lru_w_gate_a[j], lru_b_gate_a[j], lru_w_gate_x[j],
                           lru_b_gate_x[j], lru_lambda[j], lru_w_out[j])
        h = _ffn_layer(h.reshape(B * S, D), layer, norm_pre_ffn[layer], norm_post_ffn[layer],
                       ffn_w_gate, ffn_w_up, ffn_w_down).reshape(B, S, D)
    return h
```
